```python
import math
import jax, jax.numpy as jnp
from jax import lax
import numpy as np

D_MODEL = 4096
BATCH = 4
SEQ = 2048
DEPTH = 2

N_MIXERS = 2
N_META = 16
CHUNK = 64
DK = 128
DV = 128
NK = D_MODEL // 128
NV = 2 * NK
KEY_DIM = NK * DK
VAL_DIM = NV * DV
QKV_DIM = 2 * KEY_DIM + VAL_DIM
IN_DIM = QKV_DIM + VAL_DIM + 2 * NV
CONV_K = 4
POOL_WINDOWS = (2, 4, 8, 16)
N_GROUPS = len(POOL_WINDOWS)
GROUP_W = D_MODEL // N_GROUPS
D_FF = 4 * D_MODEL
EPS = 1e-6

kernel_name = "hybrid_deltanet_pool_sqrelu"


def rmsnorm(x, g):
    x32 = x.astype(jnp.float32)
    y = x32 * lax.rsqrt(jnp.mean(x32 * x32, axis=-1, keepdims=True) + EPS)
    return (y * g.astype(jnp.float32)).astype(x.dtype)


def l2norm(x):
    return x * lax.rsqrt(jnp.sum(x * x, axis=-1, keepdims=True) + EPS)


def causal_depthwise_conv(x, w):
    c = x.shape[-1]
    kern = w.astype(x.dtype)[:, None, :]
    return lax.conv_general_dilated(x, kern, window_strides=(1,), padding=[(CONV_K - 1, 0)],
                                    dimension_numbers=("NWC", "WIO", "NWC"), feature_group_count=c)


def chunk_gated_delta_rule(q, k, v, g, beta):
    bsz, seq_len, heads, _ = q.shape
    pad = (-N_META) % CHUNK
    fp = lambda t: jnp.pad(t, [(0, 0), (pad, 0)] + [(0, 0)] * (t.ndim - 2))
    q, k, v, g, beta = (fp(t) for t in (q, k, v, g, beta))
    n = (seq_len + pad) // CHUNK

    def to_chunks(t):
        t = t.reshape((bsz, n, CHUNK, heads) + t.shape[3:])
        return jnp.moveaxis(t, 3, 1)

    q = to_chunks(q) * (DK ** -0.5)
    k, v, g, beta = (to_chunks(t) for t in (k, v, g, beta))
    k_beta = k * beta[..., None]
    v_beta = v * beta[..., None]
    gc = jnp.cumsum(g, axis=-1)

    lower = jnp.tril(jnp.ones((CHUNK, CHUNK), dtype=bool))
    strict = jnp.tril(jnp.ones((CHUNK, CHUNK), dtype=bool), -1)
    diff = gc[..., :, None] - gc[..., None, :]
    decay = jnp.where(lower, jnp.exp(jnp.where(lower, diff, 0.0)), 0.0)

    a_mat = jnp.where(strict, jnp.einsum("bhncd,bhnsd->bhncs", k_beta, k) * decay, 0.0)
    eye = jnp.eye(CHUNK, dtype=jnp.float32)
    t_mat = lax.linalg.triangular_solve(eye + a_mat, jnp.broadcast_to(eye, a_mat.shape),
                                        left_side=True, lower=True, unit_diagonal=True)
    u = jnp.einsum("bhncs,bhnse->bhnce", t_mat, v_beta)
    w = jnp.einsum("bhncs,bhnsd->bhncd", t_mat, k_beta * jnp.exp(gc)[..., None])

    qk = jnp.where(lower, jnp.einsum("bhncd,bhnsd->bhncs", q, k) * decay, 0.0)
    q_dec = q * jnp.exp(gc)[..., None]
    k_dec = k * jnp.exp(gc[..., -1:] - gc)[..., None]
    g_last = jnp.exp(gc[..., -1])

    xs = tuple(jnp.moveaxis(t, 2, 0) for t in (qk, q_dec, k_dec, u, w, g_last))

    def step(state, inp):
        qk_i, qd_i, kd_i, u_i, w_i, gl_i = inp
        v_new = u_i - jnp.einsum("bhcd,bhde->bhce", w_i, state)
        o_i = jnp.einsum("bhcd,bhde->bhce", qd_i, state) + jnp.einsum("bhcs,bhse->bhce", qk_i, v_new)
        state = state * gl_i[..., None, None] + jnp.einsum("bhcd,bhce->bhde", kd_i, v_new)
        return state, o_i

    s0 = jnp.zeros((bsz, heads, DK, DV), jnp.float32)
    _, o = lax.scan(step, s0, xs)
    o = jnp.moveaxis(jnp.moveaxis(o, 0, 2), 1, 3).reshape(bsz, n * CHUNK, heads, DV)
    return o[:, pad:]


def gated_deltanet(h, w_in, conv_w, a_log, dt_bias, out_norm, w_out):
    bsz, seq_len, _ = h.shape
    proj = h @ w_in.astype(h.dtype)
    qkv = proj[..., :QKV_DIM]
    z = proj[..., QKV_DIM:QKV_DIM + VAL_DIM]
    b = proj[..., QKV_DIM + VAL_DIM:QKV_DIM + VAL_DIM + NV]
    a = proj[..., QKV_DIM + VAL_DIM + NV:]
    qkv = jax.nn.silu(causal_depthwise_conv(qkv, conv_w)).astype(jnp.float32)
    q = l2norm(qkv[..., :KEY_DIM].reshape(bsz, seq_len, NK, DK))
    k = l2norm(qkv[..., KEY_DIM:2 * KEY_DIM].reshape(bsz, seq_len, NK, DK))
    v = qkv[..., 2 * KEY_DIM:].reshape(bsz, seq_len, NV, DV)
    q = jnp.repeat(q, NV // NK, axis=2)
    k = jnp.repeat(k, NV // NK, axis=2)
    beta = jax.nn.sigmoid(b.astype(jnp.float32))
    g = -jnp.exp(a_log.astype(jnp.float32)) * jax.nn.softplus(a.astype(jnp.float32) + dt_bias.astype(jnp.float32))
    o = chunk_gated_delta_rule(q, k, v, g, beta)
    o = o * lax.rsqrt(jnp.mean(o * o, axis=-1, keepdims=True) + EPS) * out_norm.astype(jnp.float32)
    o = o * jax.nn.silu(z.astype(jnp.float32).reshape(bsz, seq_len, NV, DV))
    return o.reshape(bsz, seq_len, VAL_DIM).astype(h.dtype) @ w_out.astype(h.dtype)


def multiscale_pool(h, w_pool, scale):
    bsz, seq_len, _ = h.shape
    hg = h.astype(jnp.float32).reshape(bsz, seq_len, N_GROUPS, GROUP_W)
    cs = jnp.cumsum(hg, axis=1)
    pos = jnp.arange(1, seq_len + 1, dtype=jnp.float32)
    means = []
    for gi, win in enumerate(POOL_WINDOWS):
        c = cs[:, :, gi]
        lag = jnp.pad(c, ((0, 0), (win, 0), (0, 0)))[:, :seq_len]
        means.append((c - lag) / jnp.minimum(pos, float(win))[None, :, None])
    pooled = (jnp.stack(means, axis=2) - hg).astype(h.dtype)
    y = jnp.einsum("blgc,gcd->blgd", pooled, w_pool.astype(h.dtype)).reshape(bsz, seq_len, D_MODEL)
    return y * scale.astype(h.dtype)


def sq_relu_mlp(h, w_up, w_down):
    u = h @ w_up.astype(h.dtype)
    r = jnp.maximum(u, 0)
    return (r * r) @ w_down.astype(h.dtype)


def setup_inputs(seed: int = 0) -> dict:
    key = jax.random.key(seed)
    ks = jax.random.split(key, 20)
    n_a = len(range(0, DEPTH, N_MIXERS))
    n_b = len(range(1, DEPTH, N_MIXERS))
    nrm = lambda k, s: jax.random.normal(k, s, jnp.float32)
    gain = lambda k, s: 1.0 + 0.02 * nrm(k, s)
    return {
        "x": nrm(ks[0], (BATCH, SEQ, D_MODEL)),
        "meta_tokens": nrm(ks[1], (N_META, D_MODEL)),
        "mix_norm": gain(ks[2], (DEPTH, D_MODEL)),
        "dn_w_in": nrm(ks[3], (n_a, D_MODEL, IN_DIM)) * D_MODEL ** -0.5,
        "dn_conv_w": nrm(ks[4], (n_a, CONV_K, QKV_DIM)) * CONV_K ** -0.5,
        "dn_a_log": jnp.log(jax.random.uniform(ks[5], (n_a, NV), jnp.float32, 1.0, 16.0)),
        "dn_dt_bias": 0.1 * nrm(ks[6], (n_a, NV)),
        "dn_out_norm": gain(ks[7], (n_a, DV)),
        "dn_w_out": nrm(ks[8], (n_a, VAL_DIM, D_MODEL)) * VAL_DIM ** -0.5,
        "pool_w": nrm(ks[9], (n_b, N_GROUPS, GROUP_W, GROUP_W)) * GROUP_W ** -0.5,
        "pool_scale": gain(ks[10], (n_b, D_MODEL)),
        "mlp_norm": gain(ks[11], (DEPTH, D_MODEL)),
        "w_up": nrm(ks[12], (DEPTH, D_MODEL, D_FF)) * D_MODEL ** -0.5,
        "w_down": nrm(ks[13], (DEPTH, D_FF, D_MODEL)) * D_FF ** -0.5,
        "final_norm": gain(ks[14], (D_MODEL,)),
    }


def reference(x, meta_tokens, mix_norm, dn_w_in, dn_conv_w, dn_a_log, dn_dt_bias, dn_out_norm,
              dn_w_out, pool_w, pool_scale, mlp_norm, w_up, w_down, final_norm):
    bsz = x.shape[0]
    meta = jnp.broadcast_to(meta_tokens.astype(x.dtype)[None], (bsz, N_META, D_MODEL))
    h = jnp.concatenate([meta, x], axis=1)
    for i in range(DEPTH):
        j = i // N_MIXERS
        hn = rmsnorm(h, mix_norm[i])
        if i % N_MIXERS == 0:
            mix = gated_deltanet(hn, dn_w_in[j], dn_conv_w[j], dn_a_log[j], dn_dt_bias[j],
                                 dn_out_norm[j], dn_w_out[j])
        else:
            mix = multiscale_pool(hn, pool_w[j], pool_scale[j])
        h = h + mix.astype(h.dtype)
        h = h + sq_relu_mlp(rmsnorm(h, mlp_norm[i]), w_up[i], w_down[i]).astype(h.dtype)
    h = rmsnorm(h, final_norm)
    return h[:, N_META:]
```

```python
import functools

import jax
import jax.numpy as jnp
from jax import lax
from jax.experimental import pallas as pl
from jax.experimental.pallas import tpu as pltpu

F32 = jnp.float32
BF16 = jnp.bfloat16

N_META = 16
CHUNK = 64
DK = 128
DV = 128
CONV_K = 4
POOL_WINDOWS = (2, 4, 8, 16)
EPS = 1e-6
PAD = (-N_META) % CHUNK
LOOKBACK = 16

V7X_VMEM_BYTES = 64 * 1024 * 1024
MIB = 1024 * 1024


def _params(vmem_bytes, n_axes):
    limit = min(int(vmem_bytes) + 4 * MIB, V7X_VMEM_BYTES - 6 * MIB)
    return pltpu.CompilerParams(dimension_semantics=("arbitrary",) * n_axes, vmem_limit_bytes=limit)


def _rmsnorm_kernel(x_ref, g_ref, o_ref):
    x = x_ref[...]
    ms = jnp.mean(x * x, axis=-1, keepdims=True)
    o_ref[...] = (x * lax.rsqrt(ms + EPS) * g_ref[...]).astype(o_ref.dtype)


def _rmsnorm(x, g, out_dtype, tr=128):
    m, d = x.shape
    return pl.pallas_call(
        _rmsnorm_kernel,
        grid=(m // tr,),
        in_specs=[pl.BlockSpec((tr, d), lambda i: (i, 0)), pl.BlockSpec((1, d), lambda i: (0, 0))],
        out_specs=pl.BlockSpec((tr, d), lambda i: (i, 0)),
        out_shape=jax.ShapeDtypeStruct((m, d), out_dtype),
        compiler_params=_params(6 * tr * d * 4, 1),
        name="rmsnorm",
    )(x, g.reshape(1, d))


def _final_norm(h3, g, seq):
    b, lp, d = h3.shape
    tr = CHUNK
    skip = (lp - seq) // tr
    return pl.pallas_call(
        _rmsnorm_kernel,
        grid=(b, seq // tr),
        in_specs=[pl.BlockSpec((None, tr, d), lambda bi, i: (bi, i + skip, 0)),
                  pl.BlockSpec((1, d), lambda bi, i: (0, 0))],
        out_specs=pl.BlockSpec((None, tr, d), lambda bi, i: (bi, i, 0)),
        out_shape=jax.ShapeDtypeStruct((b, seq, d), h3.dtype),
        compiler_params=_params(6 * tr * d * 4, 2),
        name="final_norm",
    )(h3, g.reshape(1, d))


def _mm_kernel(*refs, nk, tm, relu2, has_res, has_scale):
    it = iter(refs)
    a_ref = next(it)
    w_ref = next(it)
    res_ref = next(it) if has_res else None
    scale_ref = next(it) if has_scale else None
    o_ref = next(it)
    wb_ref = next(it)
    acc_ref = next(it) if nk > 1 else None
    k = pl.program_id(1)
    i = pl.program_id(2)

    @pl.when(i == 0)
    def _cast_weights():
        wb_ref[...] = w_ref[...].astype(BF16)

    d = jnp.dot(a_ref[...], wb_ref[...], preferred_element_type=F32)

    def finish(y):
        if relu2:
            r = jnp.maximum(y, 0.0)
            y = r * r
        if has_scale:
            y = y * scale_ref[...]
        if has_res:
            y = res_ref[...] + y
        o_ref[...] = y.astype(o_ref.dtype)

    if nk == 1:
        finish(d)
    else:
        rows = pl.ds(pl.multiple_of(i * tm, 8), tm)

        @pl.when(k == 0)
        def _init():
            acc_ref[rows, :] = d

        @pl.when(k > 0)
        def _accumulate():
            acc_ref[rows, :] += d

        @pl.when(k == nk - 1)
        def _finish():
            finish(acc_ref[rows, :])


def _matmul(a, w, layer, *, n, n_off=0, tn, tk, tm, out_dtype, relu2=False, res=None):
    m, kdim = a.shape
    nk = kdim // tk
    assert m % tm == 0 and kdim % tk == 0 and n % tn == 0 and n_off % tn == 0
    joff = n_off // tn
    last = nk - 1

    def o_map(j, k, i):
        return (jnp.where(k == last, i, 0), j) if nk > 1 else (i, j)

    in_specs = [pl.BlockSpec((tm, tk), lambda j, k, i: (i, k)),
                pl.BlockSpec((None, tk, tn), lambda j, k, i: (layer, k, j + joff))]
    args = [a, w]
    if res is not None:
        in_specs.append(pl.BlockSpec((tm, tn), o_map))
        args.append(res)
    scratch = [pltpu.VMEM((tk, tn), BF16)]
    if nk > 1:
        scratch.append(pltpu.VMEM((m, tn), F32))
    osz = jnp.dtype(out_dtype).itemsize
    vmem = (2 * tk * tn * 4 + tk * tn * 2 + 2 * tm * tk * 2 + 2 * tm * tn * osz + 2 * tm * tn * 4
            + (2 * tm * tn * 4 if res is not None else 0) + (m * tn * 4 if nk > 1 else 0))
    return pl.pallas_call(
        functools.partial(_mm_kernel, nk=nk, tm=tm, relu2=relu2, has_res=res is not None, has_scale=False),
        grid=(n // tn, nk, m // tm),
        in_specs=in_specs,
        out_specs=pl.BlockSpec((tm, tn), o_map),
        out_shape=jax.ShapeDtypeStruct((m, n), out_dtype),
        scratch_shapes=scratch,
        compiler_params=_params(vmem, 3),
        name="matmul",
    )(*args)


def _pool_matmul(pooled, pool_w, layer, scale, res, *, tm):
    m, d = pooled.shape
    n_groups, gw = pool_w.shape[1], pool_w.shape[2]
    vmem = 2 * gw * gw * 4 + gw * gw * 2 + 2 * tm * gw * 2 + 6 * tm * gw * 4
    return pl.pallas_call(
        functools.partial(_mm_kernel, nk=1, tm=tm, relu2=False, has_res=True, has_scale=True),
        grid=(n_groups, 1, m // tm),
        in_specs=[pl.BlockSpec((tm, gw), lambda j, k, i: (i, j)),
                  pl.BlockSpec((None, None, gw, gw), lambda j, k, i: (layer, j, 0, 0)),
                  pl.BlockSpec((tm, gw), lambda j, k, i: (i, j)),
                  pl.BlockSpec((1, gw), lambda j, k, i: (0, j))],
        out_specs=pl.BlockSpec((tm, gw), lambda j, k, i: (i, j)),
        out_shape=jax.ShapeDtypeStruct((m, d), res.dtype),
        scratch_shapes=[pltpu.VMEM((gw, gw), BF16)],
        compiler_params=_params(vmem, 3),
        name="pool_matmul",
    )(pooled, pool_w, res, scale.reshape(1, d))


def _gate_kernel(ba_ref, alog_ref, dtb_ref, o_ref, *, nv):
    n_chunks = ba_ref.shape[0] // CHUNK
    lane = lax.broadcasted_iota(jnp.int32, (CHUNK, 2 * nv), 1)
    pos = lax.broadcasted_iota(jnp.int32, (CHUNK, 2 * nv), 0)
    neg_a = -jnp.exp(alog_ref[...])
    dtb = dtb_ref[...]

    def body(c, carry):
        rows = pl.ds(pl.multiple_of(c * CHUNK, CHUNK), CHUNK)
        x = ba_ref[rows, :]
        beta = jax.nn.sigmoid(x)
        xa = x + dtb
        g = neg_a * (jnp.maximum(xa, 0.0) + jnp.log1p(jnp.exp(-jnp.abs(xa))))
        val = jnp.where(lane < nv, beta, g)
        val = jnp.where(jnp.logical_or(c > 0, pos >= PAD), val, 0.0)
        cs = val
        s = 1
        while s < CHUNK:
            cs = cs + jnp.where(pos >= s, pltpu.roll(cs, s, 0), 0.0)
            s *= 2
        o_ref[rows, :] = jnp.where(lane < nv, val, cs)
        return carry

    lax.fori_loop(0, n_chunks, body, 0)


def _gates(ba3, a_log, dt_bias):
    b, lp, w = ba3.shape
    nv = w // 2
    zeros = jnp.zeros((nv,), F32)
    alog = jnp.concatenate([zeros, a_log.astype(F32)]).reshape(1, w)
    dtb = jnp.concatenate([zeros, dt_bias.astype(F32)]).reshape(1, w)
    return pl.pallas_call(
        functools.partial(_gate_kernel, nv=nv),
        grid=(b,),
        in_specs=[pl.BlockSpec((None, lp, w), lambda bi: (bi, 0, 0)),
                  pl.BlockSpec((1, w), lambda bi: (0, 0)),
                  pl.BlockSpec((1, w), lambda bi: (0, 0))],
        out_specs=pl.BlockSpec((None, lp, w), lambda bi: (bi, 0, 0)),
        out_shape=jax.ShapeDtypeStruct((b, lp, w), F32),
        compiler_params=_params(8 * lp * w * 4, 1),
        name="delta_gates",
    )(ba3, alog, dtb)


def _delta_kernel(q_ref, k_ref, v_ref, z_ref, cwq_ref, cwk_ref, cwv_ref, gcb_ref, gcr_ref, onorm_ref,
                  o_ref, u_s, w_s, qd_s, kd_s, qk_s, gl_s, st_s, *, nv):
    kh = pl.program_id(1)
    n_chunks = q_ref.shape[0] // CHUNK
    heads = v_ref.shape[1] // DV
    row_i = lax.broadcasted_iota(jnp.int32, (CHUNK, CHUNK), 0)
    col_i = lax.broadcasted_iota(jnp.int32, (CHUNK, CHUNK), 1)
    lower = row_i >= col_i
    strict = row_i > col_i
    eye = jnp.where(row_i == col_i, 1.0, 0.0).astype(F32)
    lane = lax.broadcasted_iota(jnp.int32, (CHUNK, 2 * nv), 1)
    nt_dims = (((1,), (1,)), ((), ()))
    tn_dims = (((0,), (0,)), ((), ()))

    def mm(x, y):
        return jnp.dot(x.astype(BF16), y.astype(BF16), preferred_element_type=F32)

    def l2n(t):
        return t * lax.rsqrt(jnp.sum(t * t, axis=-1, keepdims=True) + EPS)

    def chunk_local(c, carry):
        r0 = pl.multiple_of(c * CHUNK, CHUNK)
        lb0 = pl.multiple_of(jnp.maximum(r0 - LOOKBACK, 0), LOOKBACK)
        rows = pl.ds(r0, CHUNK)

        def conv_silu(ref, cw_ref):
            x = jnp.concatenate([ref[pl.ds(lb0, LOOKBACK), :], ref[rows, :]], axis=0).astype(F32)
            w = cw_ref[...]
            y = x[LOOKBACK:] * w[CONV_K - 1:CONV_K]
            for j in range(1, CONV_K):
                y = y + x[LOOKBACK - j:LOOKBACK - j + CHUNK] * w[CONV_K - 1 - j:CONV_K - j]
            return y * jax.nn.sigmoid(y)

        q = l2n(conv_silu(q_ref, cwq_ref)) * (DK ** -0.5)
        k = l2n(conv_silu(k_ref, cwk_ref))
        v = conv_silu(v_ref, cwv_ref)
        qb = q.astype(BF16)
        kb = k.astype(BF16)
        kk = lax.dot_general(kb, kb, nt_dims, preferred_element_type=F32)
        qk = lax.dot_general(qb, kb, nt_dims, preferred_element_type=F32)
        gcb = gcb_ref[rows, :]
        for j in range(heads):
            h = heads * kh + j
            beta = jnp.sum(jnp.where(lane == h, gcb, 0.0), axis=1, keepdims=True)
            gc = jnp.sum(jnp.where(lane == h + nv, gcb, 0.0), axis=1, keepdims=True)
            gcr = gcr_ref[j, pl.ds(c, 1), :]
            gl = gcr[:, CHUNK - 1:CHUNK]
            decay = jnp.where(lower, jnp.exp(jnp.where(lower, gc - gcr, 0.0)), 0.0)
            n = -jnp.where(strict, (beta * kk) * decay, 0.0)
            t = eye + n
            p = n
            s = 2
            while s < CHUNK:
                p = mm(p, p)
                t = t + mm(p, t)
                s *= 2
            egc = jnp.exp(gc)
            rhs = jnp.concatenate([v[:, j * DV:(j + 1) * DV] * beta, k * (beta * egc)], axis=1)
            uw = mm(t, rhs)
            u_s[j, rows, :] = uw[:, :DV]
            w_s[j, rows, :] = uw[:, DV:].astype(BF16)
            qk_s[j, rows, :] = (qk * decay).astype(BF16)
            qd_s[j, rows, :] = (q * egc).astype(BF16)
            kd_s[j, rows, :] = (k * jnp.exp(gl - gc)).astype(BF16)
            gl_s[j, pl.ds(c, 1), :] = jnp.broadcast_to(jnp.exp(gl), (1, DV))
        return carry

    lax.fori_loop(0, n_chunks, chunk_local, 0)

    st_s[...] = jnp.zeros_like(st_s)
    onorm = onorm_ref[...]

    def chunk_scan(c, carry):
        rows = pl.ds(pl.multiple_of(c * CHUNK, CHUNK), CHUNK)
        for j in range(heads):
            s = st_s[j]
            sb = s.astype(BF16)
            v_new = u_s[j, rows, :] - jnp.dot(w_s[j, rows, :], sb, preferred_element_type=F32)
            vb = v_new.astype(BF16)
            o = (jnp.dot(qd_s[j, rows, :], sb, preferred_element_type=F32)
                 + jnp.dot(qk_s[j, rows, :], vb, preferred_element_type=F32))
            st_s[j] = s * gl_s[j, pl.ds(c, 1), :] + lax.dot_general(
                kd_s[j, rows, :], vb, tn_dims, preferred_element_type=F32)
            on = o * lax.rsqrt(jnp.mean(o * o, axis=-1, keepdims=True) + EPS) * onorm
            z = z_ref[rows, j * DV:(j + 1) * DV].astype(F32)
            o_ref[rows, j * DV:(j + 1) * DV] = (on * (z * jax.nn.sigmoid(z))).astype(o_ref.dtype)
        return carry

    lax.fori_loop(0, n_chunks, chunk_scan, 0)


def _delta_rule(proj3, conv_w, layer, gcb, gcr, out_norm, *, nk_heads, nv):
    b, lp, _ = proj3.shape
    heads = nv // nk_heads
    vw = heads * DV
    key_dim = nk_heads * DK
    k_blk = key_dim // DK
    v_blk = 2 * key_dim // vw
    z_blk = (2 * key_dim + nv * DV) // vw
    n_chunks = lp // CHUNK
    scratch = [pltpu.VMEM((heads, lp, DV), F32),
               pltpu.VMEM((heads, lp, DK), BF16),
               pltpu.VMEM((heads, lp, DK), BF16),
               pltpu.VMEM((heads, lp, DK), BF16),
               pltpu.VMEM((heads, lp, CHUNK), BF16),
               pltpu.VMEM((heads, n_chunks, DV), F32),
               pltpu.VMEM((heads, DK, DV), F32)]
    vmem = (2 * lp * (2 * DK + 2 * vw) * 2 + 2 * lp * 2 * nv * 4 + 2 * lp * vw * 2
            + heads * lp * (DV * 4 + 3 * DK * 2 + 128 * 2) + 8 * MIB)
    return pl.pallas_call(
        functools.partial(_delta_kernel, nv=nv),
        grid=(b, nk_heads),
        in_specs=[pl.BlockSpec((None, lp, DK), lambda bi, kh: (bi, 0, kh)),
                  pl.BlockSpec((None, lp, DK), lambda bi, kh: (bi, 0, k_blk + kh)),
                  pl.BlockSpec((None, lp, vw), lambda bi, kh: (bi, 0, v_blk + kh)),
                  pl.BlockSpec((None, lp, vw), lambda bi, kh: (bi, 0, z_blk + kh)),
                  pl.BlockSpec((None, CONV_K, DK), lambda bi, kh: (layer, 0, kh)),
                  pl.BlockSpec((None, CONV_K, DK), lambda bi, kh: (layer, 0, k_blk + kh)),
                  pl.BlockSpec((None, CONV_K, vw), lambda bi, kh: (layer, 0, v_blk + kh)),
                  pl.BlockSpec((None, lp, 2 * nv), lambda bi, kh: (bi, 0, 0)),
                  pl.BlockSpec((None, heads, n_chunks, CHUNK), lambda bi, kh: (bi, kh, 0, 0)),
                  pl.BlockSpec((1, DV), lambda bi, kh: (0, 0))],
        out_specs=pl.BlockSpec((None, lp, vw), lambda bi, kh: (bi, 0, kh)),
        out_shape=jax.ShapeDtypeStruct((b, lp, nv * DV), BF16),
        scratch_shapes=scratch,
        compiler_params=_params(vmem, 2),
        name="delta_rule",
    )(proj3, proj3, proj3, proj3, conv_w, conv_w, conv_w, gcb, gcr, out_norm.reshape(1, DV))


def _pool_kernel(lb_ref, x_ref, g_ref, o_ref, *, col_blk):
    i = pl.program_id(1)
    tr, d = x_ref.shape
    gw = d // len(POOL_WINDOWS)
    x = x_ref[...]
    lb = lb_ref[...]
    inv = lax.rsqrt(jnp.mean(x * x, axis=-1, keepdims=True) + EPS)
    inv_lb = lax.rsqrt(jnp.mean(lb * lb, axis=-1, keepdims=True) + EPS)
    row = i * tr - LOOKBACK + lax.broadcasted_iota(jnp.int32, (tr + LOOKBACK, 1), 0)
    valid = row >= PAD
    pos = (row[LOOKBACK:] - (PAD - 1)).astype(F32)
    for gi, win in enumerate(POOL_WINDOWS):
        div = jnp.where(valid[LOOKBACK:], jnp.minimum(pos, float(win)), 1.0)
        for cb in range(gw // col_blk):
            c0 = gi * gw + cb * col_blk
            gv = g_ref[:, c0:c0 + col_blk]
            xe = jnp.concatenate([lb[:, c0:c0 + col_blk] * inv_lb * gv, x[:, c0:c0 + col_blk] * inv * gv], axis=0)
            xe = jnp.where(valid, xe, 0.0)
            s = xe
            sh = 1
            while sh < win:
                s = s + pltpu.roll(s, sh, 0)
                sh *= 2
            o_ref[:, c0:c0 + col_blk] = (s[LOOKBACK:] / div - xe[LOOKBACK:]).astype(o_ref.dtype)


def _pool(h3, g):
    b, lp, d = h3.shape
    tr = CHUNK
    per = tr // LOOKBACK
    return pl.pallas_call(
        functools.partial(_pool_kernel, col_blk=512),
        grid=(b, lp // tr),
        in_specs=[pl.BlockSpec((None, LOOKBACK, d), lambda bi, i: (bi, jnp.maximum(i * per - 1, 0), 0)),
                  pl.BlockSpec((None, tr, d), lambda bi, i: (bi, i, 0)),
                  pl.BlockSpec((1, d), lambda bi, i: (0, 0))],
        out_specs=pl.BlockSpec((None, tr, d), lambda bi, i: (bi, i, 0)),
        out_shape=jax.ShapeDtypeStruct((b, lp, d), BF16),
        compiler_params=_params(8 * tr * d * 4, 2),
        name="pool",
    )(h3, h3, g.reshape(1, d))


def kernel(x, meta_tokens, mix_norm, dn_w_in, dn_conv_w, dn_a_log, dn_dt_bias, dn_out_norm, dn_w_out, pool_w,
           pool_scale, mlp_norm, w_up, w_down, final_norm):
    b, seq, d = x.shape
    nv = dn_a_log.shape[1]
    val_dim = nv * DV
    key_dim = (dn_conv_w.shape[2] - val_dim) // 2
    nk_heads = key_dim // DK
    qkvz = 2 * key_dim + 2 * val_dim
    d_ff = w_up.shape[2]
    depth = mix_norm.shape[0]
    lp = PAD + N_META + seq
    m = b * lp
    tm = m // 8
    assert lp % CHUNK == 0 and tm % 16 == 0 and dn_w_in.shape[2] == qkvz + 2 * nv

    meta = jnp.broadcast_to(meta_tokens.astype(x.dtype)[None], (b, N_META, d))
    h = jnp.concatenate([jnp.zeros((b, PAD, d), x.dtype), meta, x], axis=1).reshape(m, d)

    for i in range(depth):
        j = i // 2
        if i % 2 == 0:
            hn = _rmsnorm(h, mix_norm[i], BF16)
            proj = _matmul(hn, dn_w_in, j, n=qkvz, tn=512, tk=d, tm=tm, out_dtype=BF16)
            ba = _matmul(hn, dn_w_in, j, n=2 * nv, n_off=qkvz, tn=2 * nv, tk=d, tm=tm, out_dtype=F32)
            gcb = _gates(ba.reshape(b, lp, 2 * nv), dn_a_log[j], dn_dt_bias[j])
            gcr = gcb[:, :, nv:].reshape(b, lp // CHUNK, CHUNK, nv).transpose(0, 3, 1, 2)
            og = _delta_rule(proj.reshape(b, lp, qkvz), dn_conv_w, j, gcb, gcr, dn_out_norm[j],
                             nk_heads=nk_heads, nv=nv)
            h = _matmul(og.reshape(m, val_dim), dn_w_out, j, n=d, tn=512, tk=2048, tm=tm, out_dtype=F32, res=h)
        else:
            pooled = _pool(h.reshape(b, lp, d), mix_norm[i])
            h = _pool_matmul(pooled.reshape(m, d), pool_w, j, pool_scale[j], h, tm=tm)
        hn = _rmsnorm(h, mlp_norm[i], BF16)
        act = _matmul(hn, w_up, i, n=d_ff, tn=512, tk=d, tm=tm, out_dtype=BF16, relu2=True)
        h = _matmul(act, w_down, i, n=d, tn=512, tk=2048, tm=tm, out_dtype=F32, res=h)

    return _final_norm(h.reshape(b, lp, d), final_norm, seq)
```

```python
import functools

import jax
import jax.numpy as jnp
from jax import lax
from jax.experimental import pallas as pl
from jax.experimental.pallas import tpu as pltpu

F32 = jnp.float32
BF16 = jnp.bfloat16

N_META = 16
CHUNK = 64
DK = 128
DV = 128
CONV_K = 4
POOL_WINDOWS = (2, 4, 8, 16)
EPS = 1e-6
PAD = (-N_META) % CHUNK
LOOKBACK = 16
HEADS_PER_KEY = 2
PAIR_W = HEADS_PER_KEY * CHUNK

V7X_VMEM_BYTES = 64 * 1024 * 1024
MIB = 1024 * 1024


def _params(vmem_bytes, n_axes):
    limit = min(int(vmem_bytes) + 4 * MIB, V7X_VMEM_BYTES - 6 * MIB)
    return pltpu.CompilerParams(dimension_semantics=("arbitrary",) * n_axes, vmem_limit_bytes=limit)


def _rmsnorm_kernel(x_ref, g_ref, o_ref):
    x = x_ref[...]
    ms = jnp.mean(x * x, axis=-1, keepdims=True)
    o_ref[...] = (x * lax.rsqrt(ms + EPS) * g_ref[...]).astype(o_ref.dtype)


def _rmsnorm(x, g, out_dtype, tr=128):
    m, d = x.shape
    return pl.pallas_call(
        _rmsnorm_kernel,
        grid=(m // tr,),
        in_specs=[pl.BlockSpec((tr, d), lambda i: (i, 0)), pl.BlockSpec((1, d), lambda i: (0, 0))],
        out_specs=pl.BlockSpec((tr, d), lambda i: (i, 0)),
        out_shape=jax.ShapeDtypeStruct((m, d), out_dtype),
        compiler_params=_params(6 * tr * d * 4, 1),
        name="rmsnorm",
    )(x, g.reshape(1, d))


def _final_norm(h3, g, seq):
    b, lp, d = h3.shape
    tr = CHUNK
    skip = (lp - seq) // tr
    return pl.pallas_call(
        _rmsnorm_kernel,
        grid=(b, seq // tr),
        in_specs=[pl.BlockSpec((None, tr, d), lambda bi, i: (bi, i + skip, 0)),
                  pl.BlockSpec((1, d), lambda bi, i: (0, 0))],
        out_specs=pl.BlockSpec((None, tr, d), lambda bi, i: (bi, i, 0)),
        out_shape=jax.ShapeDtypeStruct((b, seq, d), h3.dtype),
        compiler_params=_params(6 * tr * d * 4, 2),
        name="final_norm",
    )(h3, g.reshape(1, d))


def _mm_kernel(*refs, nk, tm, relu2, has_res, has_scale):
    it = iter(refs)
    a_ref = next(it)
    w_ref = next(it)
    res_ref = next(it) if has_res else None
    scale_ref = next(it) if has_scale else None
    o_ref = next(it)
    wb_ref = next(it)
    acc_ref = next(it) if nk > 1 else None
    k = pl.program_id(1)
    i = pl.program_id(2)

    @pl.when(i == 0)
    def _cast_weights():
        wb_ref[...] = w_ref[...].astype(BF16)

    d = jnp.dot(a_ref[...], wb_ref[...], preferred_element_type=F32)

    def finish(y):
        if relu2:
            r = jnp.maximum(y, 0.0)
            y = r * r
        if has_scale:
            y = y * scale_ref[...]
        if has_res:
            y = res_ref[...] + y
        o_ref[...] = y.astype(o_ref.dtype)

    if nk == 1:
        finish(d)
    else:
        rows = pl.ds(pl.multiple_of(i * tm, 8), tm)

        @pl.when(k == 0)
        def _init():
            acc_ref[rows, :] = d

        @pl.when(k > 0)
        def _accumulate():
            acc_ref[rows, :] += d

        @pl.when(k == nk - 1)
        def _finish():
            finish(acc_ref[rows, :])


def _matmul(a, w, layer, *, n, n_off=0, tn, tk, tm, out_dtype, relu2=False, res=None):
    m, kdim = a.shape
    nk = kdim // tk
    assert m % tm == 0 and kdim % tk == 0 and n % tn == 0 and n_off % tn == 0
    joff = n_off // tn
    last = nk - 1

    def o_map(j, k, i):
        return (jnp.where(k == last, i, 0), j) if nk > 1 else (i, j)

    in_specs = [pl.BlockSpec((tm, tk), lambda j, k, i: (i, k)),
                pl.BlockSpec((None, tk, tn), lambda j, k, i: (layer, k, j + joff))]
    args = [a, w]
    if res is not None:
        in_specs.append(pl.BlockSpec((tm, tn), o_map))
        args.append(res)
    scratch = [pltpu.VMEM((tk, tn), BF16)]
    if nk > 1:
        scratch.append(pltpu.VMEM((m, tn), F32))
    osz = jnp.dtype(out_dtype).itemsize
    vmem = (2 * tk * tn * 4 + tk * tn * 2 + 2 * tm * tk * 2 + 2 * tm * tn * osz + 2 * tm * tn * 4
            + (2 * tm * tn * 4 if res is not None else 0) + (m * tn * 4 if nk > 1 else 0))
    return pl.pallas_call(
        functools.partial(_mm_kernel, nk=nk, tm=tm, relu2=relu2, has_res=res is not None, has_scale=False),
        grid=(n // tn, nk, m // tm),
        in_specs=in_specs,
        out_specs=pl.BlockSpec((tm, tn), o_map),
        out_shape=jax.ShapeDtypeStruct((m, n), out_dtype),
        scratch_shapes=scratch,
        compiler_params=_params(vmem, 3),
        name="matmul",
    )(*args)


def _pool_matmul(pooled, pool_w, layer, scale, res, *, tm):
    m, d = pooled.shape
    n_groups, gw = pool_w.shape[1], pool_w.shape[2]
    vmem = 2 * gw * gw * 4 + gw * gw * 2 + 2 * tm * gw * 2 + 6 * tm * gw * 4
    return pl.pallas_call(
        functools.partial(_mm_kernel, nk=1, tm=tm, relu2=False, has_res=True, has_scale=True),
        grid=(n_groups, 1, m // tm),
        in_specs=[pl.BlockSpec((tm, gw), lambda j, k, i: (i, j)),
                  pl.BlockSpec((None, None, gw, gw), lambda j, k, i: (layer, j, 0, 0)),
                  pl.BlockSpec((tm, gw), lambda j, k, i: (i, j)),
                  pl.BlockSpec((1, gw), lambda j, k, i: (0, j))],
        out_specs=pl.BlockSpec((tm, gw), lambda j, k, i: (i, j)),
        out_shape=jax.ShapeDtypeStruct((m, d), res.dtype),
        scratch_shapes=[pltpu.VMEM((gw, gw), BF16)],
        compiler_params=_params(vmem, 3),
        name="pool_matmul",
    )(pooled, pool_w, res, scale.reshape(1, d))


def _gate_kernel(ba_ref, alog_ref, dtb_ref, o_ref, *, nv):
    n_chunks = ba_ref.shape[0] // CHUNK
    lane = lax.broadcasted_iota(jnp.int32, (CHUNK, 2 * nv), 1)
    pos = lax.broadcasted_iota(jnp.int32, (CHUNK, 2 * nv), 0)
    neg_a = -jnp.exp(alog_ref[...])
    dtb = dtb_ref[...]

    def body(c, carry):
        rows = pl.ds(pl.multiple_of(c * CHUNK, CHUNK), CHUNK)
        x = ba_ref[rows, :]
        beta = jax.nn.sigmoid(x)
        xa = x + dtb
        g = neg_a * (jnp.maximum(xa, 0.0) + jnp.log1p(jnp.exp(-jnp.abs(xa))))
        val = jnp.where(lane < nv, beta, g)
        val = jnp.where(jnp.logical_or(c > 0, pos >= PAD), val, 0.0)
        cs = val
        s = 1
        while s < CHUNK:
            cs = cs + jnp.where(pos >= s, pltpu.roll(cs, s, 0), 0.0)
            s *= 2
        o_ref[rows, :] = jnp.where(lane < nv, val, cs)
        return carry

    lax.fori_loop(0, n_chunks, body, 0)


def _gates(ba3, a_log, dt_bias):
    b, lp, w = ba3.shape
    nv = w // 2
    zeros = jnp.zeros((nv,), F32)
    alog = jnp.concatenate([zeros, a_log.astype(F32)]).reshape(1, w)
    dtb = jnp.concatenate([zeros, dt_bias.astype(F32)]).reshape(1, w)
    return pl.pallas_call(
        functools.partial(_gate_kernel, nv=nv),
        grid=(b,),
        in_specs=[pl.BlockSpec((None, lp, w), lambda bi: (bi, 0, 0)),
                  pl.BlockSpec((1, w), lambda bi: (0, 0)),
                  pl.BlockSpec((1, w), lambda bi: (0, 0))],
        out_specs=pl.BlockSpec((None, lp, w), lambda bi: (bi, 0, 0)),
        out_shape=jax.ShapeDtypeStruct((b, lp, w), F32),
        compiler_params=_params(8 * lp * w * 4, 1),
        name="delta_gates",
    )(ba3, alog, dtb)


def _gate_rows(gcb, nv):
    b, lp, _ = gcb.shape
    nkh = nv // HEADS_PER_KEY
    nc = lp // CHUNK
    gcr = gcb[:, :, nv:].reshape(b, nc, CHUNK, nkh, HEADS_PER_KEY).transpose(0, 3, 1, 4, 2)
    gl = gcr[..., CHUNK - 1:]
    return jnp.concatenate(
        [gcr.reshape(b, nkh, nc, PAIR_W),
         jnp.broadcast_to(gl, (b, nkh, nc, HEADS_PER_KEY, CHUNK)).reshape(b, nkh, nc, PAIR_W),
         jnp.broadcast_to(gl, (b, nkh, nc, HEADS_PER_KEY, DV)).reshape(b, nkh, nc, HEADS_PER_KEY * DV)], axis=-1)


def _block_diag(a, b):
    z = jnp.zeros_like(a)
    return jnp.concatenate([jnp.concatenate([a, z], axis=1), jnp.concatenate([z, b], axis=1)], axis=0)


def _delta_kernel(q_ref, k_ref, v_ref, z_ref, cwq_ref, cwk_ref, cwv_ref, gcb_ref, rows_ref, onorm_ref,
                  o_ref, u_s, wq_s, qk_s, kb_s, edg_s, st_s, *, nv, keys, group):
    kp = pl.program_id(1)
    n_chunks = q_ref.shape[0] // CHUNK
    vw = HEADS_PER_KEY * DV
    rowi = lax.broadcasted_iota(jnp.int32, (CHUNK, PAIR_W), 0)
    lanei = lax.broadcasted_iota(jnp.int32, (CHUNK, PAIR_W), 1)
    coli = lanei & (CHUNK - 1)
    left = lanei < CHUNK
    lower = rowi >= coli
    strict = rowi > coli
    eye = jnp.where(rowi == coli, 1.0, 0.0).astype(F32)
    lane_g = lax.broadcasted_iota(jnp.int32, (CHUNK, 2 * nv), 1)
    nt_dims = (((1,), (1,)), ((), ()))
    tn_dims = (((0,), (0,)), ((), ()))

    def mm_pair(x, y):
        ybd = jnp.concatenate([jnp.where(left, y, 0.0), jnp.where(left, 0.0, y)], axis=0).astype(BF16)
        return jnp.dot(x.astype(BF16), ybd, preferred_element_type=F32)

    def l2n(t):
        return t * lax.rsqrt(jnp.sum(t * t, axis=-1, keepdims=True) + EPS)

    def conv_silu(ref, cw_ref, rows, lb_rows, c0, width):
        x = jnp.concatenate([ref[lb_rows, c0:c0 + width], ref[rows, c0:c0 + width]], axis=0).astype(F32)
        w = cw_ref[:, c0:c0 + width]
        y = x[LOOKBACK:] * w[CONV_K - 1:CONV_K]
        for j in range(1, CONV_K):
            y = y + x[LOOKBACK - j:LOOKBACK - j + CHUNK] * w[CONV_K - 1 - j:CONV_K - j]
        return y * jax.nn.sigmoid(y)

    def chunk_local(g, carry):
        chains = []
        for ci in range(group):
            c = g * group + ci
            r0 = pl.multiple_of(c * CHUNK, CHUNK)
            lb_rows = pl.ds(pl.multiple_of(jnp.maximum(r0 - LOOKBACK, 0), LOOKBACK), LOOKBACK)
            for e in range(keys):
                chains.append(dict(c=c, e=e, rows=pl.ds(r0, CHUNK), lb_rows=lb_rows))

        for ch in chains:
            e, rows, lb_rows = ch["e"], ch["rows"], ch["lb_rows"]
            q = l2n(conv_silu(q_ref, cwq_ref, rows, lb_rows, e * DK, DK)) * (DK ** -0.5)
            k = l2n(conv_silu(k_ref, cwk_ref, rows, lb_rows, e * DK, DK))
            qb = q.astype(BF16)
            kb = k.astype(BF16)
            qkk = lax.dot_general(jnp.concatenate([qb, kb], axis=0), jnp.concatenate([kb, kb], axis=0),
                                  nt_dims, preferred_element_type=F32)
            ch.update(q=q, k=k, kb=kb, qkk=qkk)

        for ch in chains:
            gcb = gcb_ref[ch["rows"], :]

            def column(idx, gcb=gcb):
                return jnp.sum(jnp.where(lane_g == idx, gcb, 0.0), axis=1, keepdims=True)

            h0 = HEADS_PER_KEY * (keys * kp + ch["e"])
            beta_a, beta_b = column(h0), column(h0 + 1)
            gc_a, gc_b = column(nv + h0), column(nv + h0 + 1)
            gc = jnp.where(left, gc_a, gc_b)
            rr = rows_ref[ch["e"], pl.ds(ch["c"], 1), :]
            decay = jnp.where(lower, jnp.exp(jnp.where(lower, gc - rr[:, :PAIR_W], 0.0)), 0.0)
            n = -jnp.where(strict, (jnp.where(left, beta_a, beta_b) * ch["qkk"][CHUNK:]) * decay, 0.0)
            ch.update(beta_a=beta_a, beta_b=beta_b, eg_a=jnp.exp(gc_a), eg_b=jnp.exp(gc_b), decay=decay,
                      edg=jnp.exp(rr[:, PAIR_W:2 * PAIR_W] - gc), t=eye + n, p=n)

        s = 2
        while s < CHUNK:
            for ch in chains:
                ch["p"] = mm_pair(ch["p"], ch["p"])
            for ch in chains:
                ch["t"] = ch["t"] + mm_pair(ch["p"], ch["t"])
            s *= 2

        for ch in chains:
            e, rows = ch["e"], ch["rows"]
            v = conv_silu(v_ref, cwv_ref, rows, ch["lb_rows"], e * vw, vw)
            k = ch["k"]
            rhs_a = jnp.concatenate([v[:, :DV] * ch["beta_a"], k * (ch["beta_a"] * ch["eg_a"])], axis=1)
            rhs_b = jnp.concatenate([v[:, DV:] * ch["beta_b"], k * (ch["beta_b"] * ch["eg_b"])], axis=1)
            ch["uw"] = jnp.dot(ch["t"].astype(BF16), _block_diag(rhs_a.astype(BF16), rhs_b.astype(BF16)),
                               preferred_element_type=F32)

        for ch in chains:
            e, rows, uw, q = ch["e"], ch["rows"], ch["uw"], ch["q"]
            wq0 = pl.multiple_of(ch["c"] * (2 * CHUNK), 2 * CHUNK)
            u_s[e, rows, :] = jnp.concatenate([uw[:, :DV], uw[:, 2 * DV:3 * DV]], axis=1)
            wq_s[e, pl.ds(wq0, CHUNK), :] = jnp.concatenate([uw[:, DV:2 * DV], uw[:, 3 * DV:]], axis=1).astype(BF16)
            wq_s[e, pl.ds(wq0 + CHUNK, CHUNK), :] = jnp.concatenate([q * ch["eg_a"], q * ch["eg_b"]],
                                                                     axis=1).astype(BF16)
            qk_s[e, rows, :] = (ch["qkk"][:CHUNK] * ch["decay"]).astype(BF16)
            kb_s[e, rows, :] = ch["kb"]
            edg_s[e, rows, :] = ch["edg"]
        return carry

    lax.fori_loop(0, n_chunks // group, chunk_local, 0)

    st_s[...] = jnp.zeros_like(st_s)
    onorm = onorm_ref[...]

    def chunk_scan(c, carry):
        rows = pl.ds(pl.multiple_of(c * CHUNK, CHUNK), CHUNK)
        wq_rows = pl.ds(pl.multiple_of(c * (2 * CHUNK), 2 * CHUNK), 2 * CHUNK)
        states = [st_s[e] for e in range(keys)]
        r1s, v_news, outs = [], [], []
        for e in range(keys):
            sb = states[e].astype(BF16)
            r1s.append(jnp.dot(wq_s[e, wq_rows, :], _block_diag(sb[:, :DV], sb[:, DV:]),
                               preferred_element_type=F32))
        for e in range(keys):
            v_news.append(u_s[e, rows, :] - r1s[e][:CHUNK])
        for e in range(keys):
            v_new = v_news[e]
            edg = edg_s[e, rows, :]
            edg_sw = pltpu.roll(edg, CHUNK, 1)
            dv = jnp.concatenate([v_new[:, :DV] * jnp.where(left, edg, edg_sw),
                                  v_new[:, DV:] * jnp.where(left, edg_sw, edg)], axis=1).astype(BF16)
            egl = jnp.exp(rows_ref[e, pl.ds(c, 1), 2 * PAIR_W:])
            st_s[e] = states[e] * egl + lax.dot_general(kb_s[e, rows, :], dv, tn_dims, preferred_element_type=F32)
        for e in range(keys):
            vb = v_news[e].astype(BF16)
            outs.append(r1s[e][CHUNK:] + jnp.dot(qk_s[e, rows, :], _block_diag(vb[:, :DV], vb[:, DV:]),
                                                 preferred_element_type=F32))
        for e in range(keys):
            for j in range(HEADS_PER_KEY):
                oj = outs[e][:, j * DV:(j + 1) * DV]
                on = oj * lax.rsqrt(jnp.mean(oj * oj, axis=-1, keepdims=True) + EPS) * onorm
                c0 = e * vw + j * DV
                z = z_ref[rows, c0:c0 + DV].astype(F32)
                o_ref[rows, c0:c0 + DV] = (on * (z * jax.nn.sigmoid(z))).astype(o_ref.dtype)
        return carry

    lax.fori_loop(0, n_chunks, chunk_scan, 0)


def _delta_rule(proj3, conv_w, layer, gcb, rows, out_norm, *, nk_heads, nv, keys=2, group=3):
    b, lp, _ = proj3.shape
    assert nv == HEADS_PER_KEY * nk_heads and nk_heads % keys == 0 and PAIR_W == 128
    assert (lp // CHUNK) % group == 0
    kw = keys * DK
    vw = keys * HEADS_PER_KEY * DV
    key_dim = nk_heads * DK
    k_blk = key_dim // kw
    v_blk = 2 * key_dim // vw
    z_blk = (2 * key_dim + nv * DV) // vw
    n_chunks = lp // CHUNK
    pvw = HEADS_PER_KEY * DV
    scratch = [pltpu.VMEM((keys, lp, pvw), F32),
               pltpu.VMEM((keys, 2 * lp, pvw), BF16),
               pltpu.VMEM((keys, lp, PAIR_W), BF16),
               pltpu.VMEM((keys, lp, DK), BF16),
               pltpu.VMEM((keys, lp, PAIR_W), F32),
               pltpu.VMEM((keys, DK, pvw), F32)]
    vmem = (2 * lp * (2 * kw + 3 * vw) * 2 + 2 * lp * 2 * nv * 4
            + keys * lp * (pvw * 4 + 2 * pvw * 2 + PAIR_W * 2 + DK * 2 + PAIR_W * 4) + 8 * MIB)
    return pl.pallas_call(
        functools.partial(_delta_kernel, nv=nv, keys=keys, group=group),
        grid=(b, nk_heads // keys),
        in_specs=[pl.BlockSpec((None, lp, kw), lambda bi, kp: (bi, 0, kp)),
                  pl.BlockSpec((None, lp, kw), lambda bi, kp: (bi, 0, k_blk + kp)),
                  pl.BlockSpec((None, lp, vw), lambda bi, kp: (bi, 0, v_blk + kp)),
                  pl.BlockSpec((None, lp, vw), lambda bi, kp: (bi, 0, z_blk + kp)),
                  pl.BlockSpec((None, CONV_K, kw), lambda bi, kp: (layer, 0, kp)),
                  pl.BlockSpec((None, CONV_K, kw), lambda bi, kp: (layer, 0, k_blk + kp)),
                  pl.BlockSpec((None, CONV_K, vw), lambda bi, kp: (layer, 0, v_blk + kp)),
                  pl.BlockSpec((None, lp, 2 * nv), lambda bi, kp: (bi, 0, 0)),
                  pl.BlockSpec((None, keys, n_chunks, rows.shape[-1]), lambda bi, kp: (bi, kp, 0, 0)),
                  pl.BlockSpec((1, DV), lambda bi, kp: (0, 0))],
        out_specs=pl.BlockSpec((None, lp, vw), lambda bi, kp: (bi, 0, kp)),
        out_shape=jax.ShapeDtypeStruct((b, lp, nv * DV), BF16),
        scratch_shapes=scratch,
        compiler_params=_params(vmem, 2),
        name="delta_rule",
    )(proj3, proj3, proj3, proj3, conv_w, conv_w, conv_w, gcb, rows, out_norm.reshape(1, DV))


def _pool_kernel(lb_ref, x_ref, g_ref, o_ref, *, col_blk):
    i = pl.program_id(1)
    tr, d = x_ref.shape
    gw = d // len(POOL_WINDOWS)
    x = x_ref[...]
    lb = lb_ref[...]
    inv = lax.rsqrt(jnp.mean(x * x, axis=-1, keepdims=True) + EPS)
    inv_lb = lax.rsqrt(jnp.mean(lb * lb, axis=-1, keepdims=True) + EPS)
    row = i * tr - LOOKBACK + lax.broadcasted_iota(jnp.int32, (tr + LOOKBACK, 1), 0)
    valid = row >= PAD
    pos = (row[LOOKBACK:] - (PAD - 1)).astype(F32)
    for gi, win in enumerate(POOL_WINDOWS):
        div = jnp.where(valid[LOOKBACK:], jnp.minimum(pos, float(win)), 1.0)
        for cb in range(gw // col_blk):
            c0 = gi * gw + cb * col_blk
            gv = g_ref[:, c0:c0 + col_blk]
            xe = jnp.concatenate([lb[:, c0:c0 + col_blk] * inv_lb * gv, x[:, c0:c0 + col_blk] * inv * gv], axis=0)
            xe = jnp.where(valid, xe, 0.0)
            s = xe
            sh = 1
            while sh < win:
                s = s + pltpu.roll(s, sh, 0)
                sh *= 2
            o_ref[:, c0:c0 + col_blk] = (s[LOOKBACK:] / div - xe[LOOKBACK:]).astype(o_ref.dtype)


def _pool(h3, g):
    b, lp, d = h3.shape
    tr = CHUNK
    per = tr // LOOKBACK
    return pl.pallas_call(
        functools.partial(_pool_kernel, col_blk=512),
        grid=(b, lp // tr),
        in_specs=[pl.BlockSpec((None, LOOKBACK, d), lambda bi, i: (bi, jnp.maximum(i * per - 1, 0), 0)),
                  pl.BlockSpec((None, tr, d), lambda bi, i: (bi, i, 0)),
                  pl.BlockSpec((1, d), lambda bi, i: (0, 0))],
        out_specs=pl.BlockSpec((None, tr, d), lambda bi, i: (bi, i, 0)),
        out_shape=jax.ShapeDtypeStruct((b, lp, d), BF16),
        compiler_params=_params(8 * tr * d * 4, 2),
        name="pool",
    )(h3, h3, g.reshape(1, d))


def kernel(x, meta_tokens, mix_norm, dn_w_in, dn_conv_w, dn_a_log, dn_dt_bias, dn_out_norm, dn_w_out, pool_w,
           pool_scale, mlp_norm, w_up, w_down, final_norm):
    b, seq, d = x.shape
    nv = dn_a_log.shape[1]
    val_dim = nv * DV
    key_dim = (dn_conv_w.shape[2] - val_dim) // 2
    nk_heads = key_dim // DK
    qkvz = 2 * key_dim + 2 * val_dim
    d_ff = w_up.shape[2]
    depth = mix_norm.shape[0]
    lp = PAD + N_META + seq
    m = b * lp
    tm = m // 8
    assert lp % CHUNK == 0 and tm % 16 == 0 and dn_w_in.shape[2] == qkvz + 2 * nv

    meta = jnp.broadcast_to(meta_tokens.astype(x.dtype)[None], (b, N_META, d))
    h = jnp.concatenate([jnp.zeros((b, PAD, d), x.dtype), meta, x], axis=1).reshape(m, d)

    for i in range(depth):
        j = i // 2
        if i % 2 == 0:
            hn = _rmsnorm(h, mix_norm[i], BF16)
            proj = _matmul(hn, dn_w_in, j, n=qkvz, tn=512, tk=d, tm=tm, out_dtype=BF16)
            ba = _matmul(hn, dn_w_in, j, n=2 * nv, n_off=qkvz, tn=2 * nv, tk=d, tm=tm, out_dtype=F32)
            gcb = _gates(ba.reshape(b, lp, 2 * nv), dn_a_log[j], dn_dt_bias[j])
            og = _delta_rule(proj.reshape(b, lp, qkvz), dn_conv_w, j, gcb, _gate_rows(gcb, nv), dn_out_norm[j],
                             nk_heads=nk_heads, nv=nv)
            h = _matmul(og.reshape(m, val_dim), dn_w_out, j, n=d, tn=512, tk=2048, tm=tm, out_dtype=F32, res=h)
        else:
            pooled = _pool(h.reshape(b, lp, d), mix_norm[i])
            h = _pool_matmul(pooled.reshape(m, d), pool_w, j, pool_scale[j], h, tm=tm)
        hn = _rmsnorm(h, mlp_norm[i], BF16)
        act = _matmul(hn, w_up, i, n=d_ff, tn=512, tk=d, tm=tm, out_dtype=BF16, relu2=True)
        h = _matmul(act, w_down, i, n=d, tn=512, tk=2048, tm=tm, out_dtype=F32, res=h)

    return _final_norm(h.reshape(b, lp, d), final_norm, seq)
```

```python
import functools

import jax
import jax.numpy as jnp
from jax import lax
from jax.experimental import pallas as pl
from jax.experimental.pallas import tpu as pltpu

F32 = jnp.float32
BF16 = jnp.bfloat16

N_META = 16
CHUNK = 64
DK = 128
DV = 128
CONV_K = 4
POOL_WINDOWS = (2, 4, 8, 16)
EPS = 1e-6
PAD = (-N_META) % CHUNK
LOOKBACK = 16
HEADS_PER_KEY = 2
PAIR_W = HEADS_PER_KEY * CHUNK

V7X_VMEM_BYTES = 64 * 1024 * 1024
MIB = 1024 * 1024


def _params(vmem_bytes, n_axes):
    limit = min(int(vmem_bytes) + 4 * MIB, V7X_VMEM_BYTES - 6 * MIB)
    return pltpu.CompilerParams(dimension_semantics=("arbitrary",) * n_axes, vmem_limit_bytes=limit)


def _rmsnorm_kernel(x_ref, g_ref, o_ref):
    x = x_ref[...]
    ms = jnp.mean(x * x, axis=-1, keepdims=True)
    o_ref[...] = (x * lax.rsqrt(ms + EPS) * g_ref[...]).astype(o_ref.dtype)


def _rmsnorm(x, g, out_dtype, tr=128):
    m, d = x.shape
    return pl.pallas_call(
        _rmsnorm_kernel,
        grid=(m // tr,),
        in_specs=[pl.BlockSpec((tr, d), lambda i: (i, 0)), pl.BlockSpec((1, d), lambda i: (0, 0))],
        out_specs=pl.BlockSpec((tr, d), lambda i: (i, 0)),
        out_shape=jax.ShapeDtypeStruct((m, d), out_dtype),
        compiler_params=_params(6 * tr * d * 4, 1),
        name="rmsnorm",
    )(x, g.reshape(1, d))


def _final_norm(h3, g, seq):
    b, lp, d = h3.shape
    tr = CHUNK
    skip = (lp - seq) // tr
    return pl.pallas_call(
        _rmsnorm_kernel,
        grid=(b, seq // tr),
        in_specs=[pl.BlockSpec((None, tr, d), lambda bi, i: (bi, i + skip, 0)),
                  pl.BlockSpec((1, d), lambda bi, i: (0, 0))],
        out_specs=pl.BlockSpec((None, tr, d), lambda bi, i: (bi, i, 0)),
        out_shape=jax.ShapeDtypeStruct((b, seq, d), h3.dtype),
        compiler_params=_params(6 * tr * d * 4, 2),
        name="final_norm",
    )(h3, g.reshape(1, d))


def _mm_kernel(*refs, nk, tm, relu2, has_res, has_scale):
    it = iter(refs)
    a_ref = next(it)
    w_ref = next(it)
    res_ref = next(it) if has_res else None
    scale_ref = next(it) if has_scale else None
    o_ref = next(it)
    wb_ref = next(it)
    acc_ref = next(it) if nk > 1 else None
    k = pl.program_id(1)
    i = pl.program_id(2)

    @pl.when(i == 0)
    def _cast_weights():
        wb_ref[...] = w_ref[...].astype(BF16)

    def product():
        return jnp.dot(a_ref[...], wb_ref[...], preferred_element_type=F32)

    def finish(y):
        if relu2:
            r = jnp.maximum(y, 0.0)
            y = r * r
        if has_scale:
            y = y * scale_ref[...]
        if has_res:
            y = res_ref[...] + y
        o_ref[...] = y.astype(o_ref.dtype)

    if nk == 1:
        finish(product())
    else:
        rows = pl.ds(pl.multiple_of(i * tm, 8), tm)

        @pl.when(k == 0)
        def _init():
            acc_ref[rows, :] = product()

        @pl.when(jnp.logical_and(k > 0, k < nk - 1))
        def _accumulate():
            acc_ref[rows, :] += product()

        @pl.when(k == nk - 1)
        def _finish():
            finish(acc_ref[rows, :] + product())


def _matmul(a, w, layer, *, n, n_off=0, tn, tk, tm, out_dtype, relu2=False, res=None):
    m, kdim = a.shape
    nk = kdim // tk
    assert m % tm == 0 and kdim % tk == 0 and n % tn == 0 and n_off % tn == 0
    joff = n_off // tn
    last = nk - 1

    def o_map(j, k, i):
        return (jnp.where(k == last, i, 0), j) if nk > 1 else (i, j)

    in_specs = [pl.BlockSpec((tm, tk), lambda j, k, i: (i, k)),
                pl.BlockSpec((None, tk, tn), lambda j, k, i: (layer, k, j + joff))]
    args = [a, w]
    if res is not None:
        in_specs.append(pl.BlockSpec((tm, tn), o_map))
        args.append(res)
    scratch = [pltpu.VMEM((tk, tn), BF16)]
    if nk > 1:
        scratch.append(pltpu.VMEM((m, tn), F32))
    osz = jnp.dtype(out_dtype).itemsize
    vmem = (2 * tk * tn * 4 + tk * tn * 2 + 2 * tm * tk * 2 + 2 * tm * tn * osz + 2 * tm * tn * 4
            + (2 * tm * tn * 4 if res is not None else 0) + (m * tn * 4 if nk > 1 else 0))
    return pl.pallas_call(
        functools.partial(_mm_kernel, nk=nk, tm=tm, relu2=relu2, has_res=res is not None, has_scale=False),
        grid=(n // tn, nk, m // tm),
        in_specs=in_specs,
        out_specs=pl.BlockSpec((tm, tn), o_map),
        out_shape=jax.ShapeDtypeStruct((m, n), out_dtype),
        scratch_shapes=scratch,
        compiler_params=_params(vmem, 3),
        name="matmul",
    )(*args)


def _mm_stream_kernel(a_ref, w_ref, o_ref, *, relu2):
    y = jnp.dot(a_ref[...], w_ref[...].astype(BF16), preferred_element_type=F32)
    if relu2:
        r = jnp.maximum(y, 0.0)
        y = r * r
    o_ref[...] = y.astype(o_ref.dtype)


def _matmul_stream(a, w, layer, *, n, tn, tm, out_dtype, relu2=False):
    m, kdim = a.shape
    assert m % tm == 0 and n % tn == 0
    osz = jnp.dtype(out_dtype).itemsize
    vmem = 2 * kdim * tn * 4 + kdim * tn * 2 + 2 * tm * kdim * 2 + 2 * tm * tn * osz + 2 * tm * tn * 4
    return pl.pallas_call(
        functools.partial(_mm_stream_kernel, relu2=relu2),
        grid=(m // tm, n // tn),
        in_specs=[pl.BlockSpec((tm, kdim), lambda i, j: (i, 0)),
                  pl.BlockSpec((None, kdim, tn), lambda i, j: (layer, 0, j))],
        out_specs=pl.BlockSpec((tm, tn), lambda i, j: (i, j)),
        out_shape=jax.ShapeDtypeStruct((m, n), out_dtype),
        compiler_params=_params(vmem, 2),
        name="matmul_stream",
    )(a, w)


def _pool_matmul(pooled, pool_w, layer, scale, res, *, tm):
    m, d = pooled.shape
    n_groups, gw = pool_w.shape[1], pool_w.shape[2]
    vmem = 2 * gw * gw * 4 + gw * gw * 2 + 2 * tm * gw * 2 + 6 * tm * gw * 4
    return pl.pallas_call(
        functools.partial(_mm_kernel, nk=1, tm=tm, relu2=False, has_res=True, has_scale=True),
        grid=(n_groups, 1, m // tm),
        in_specs=[pl.BlockSpec((tm, gw), lambda j, k, i: (i, j)),
                  pl.BlockSpec((None, None, gw, gw), lambda j, k, i: (layer, j, 0, 0)),
                  pl.BlockSpec((tm, gw), lambda j, k, i: (i, j)),
                  pl.BlockSpec((1, gw), lambda j, k, i: (0, j))],
        out_specs=pl.BlockSpec((tm, gw), lambda j, k, i: (i, j)),
        out_shape=jax.ShapeDtypeStruct((m, d), res.dtype),
        scratch_shapes=[pltpu.VMEM((gw, gw), BF16)],
        compiler_params=_params(vmem, 3),
        name="pool_matmul",
    )(pooled, pool_w, res, scale.reshape(1, d))


def _gate_kernel(ba_ref, alog_ref, dtb_ref, o_ref, *, nv):
    n_chunks = ba_ref.shape[0] // CHUNK
    lane = lax.broadcasted_iota(jnp.int32, (CHUNK, 2 * nv), 1)
    pos = lax.broadcasted_iota(jnp.int32, (CHUNK, 2 * nv), 0)
    neg_a = -jnp.exp(alog_ref[...])
    dtb = dtb_ref[...]

    def body(c, carry):
        rows = pl.ds(pl.multiple_of(c * CHUNK, CHUNK), CHUNK)
        x = ba_ref[rows, :]
        beta = jax.nn.sigmoid(x)
        xa = x + dtb
        g = neg_a * (jnp.maximum(xa, 0.0) + jnp.log1p(jnp.exp(-jnp.abs(xa))))
        val = jnp.where(lane < nv, beta, g)
        val = jnp.where(jnp.logical_or(c > 0, pos >= PAD), val, 0.0)
        cs = val
        s = 1
        while s < CHUNK:
            cs = cs + jnp.where(pos >= s, pltpu.roll(cs, s, 0), 0.0)
            s *= 2
        o_ref[rows, :] = jnp.where(lane < nv, val, cs)
        return carry

    lax.fori_loop(0, n_chunks, body, 0)


def _gates(ba3, a_log, dt_bias):
    b, lp, w = ba3.shape
    nv = w // 2
    zeros = jnp.zeros((nv,), F32)
    alog = jnp.concatenate([zeros, a_log.astype(F32)]).reshape(1, w)
    dtb = jnp.concatenate([zeros, dt_bias.astype(F32)]).reshape(1, w)
    return pl.pallas_call(
        functools.partial(_gate_kernel, nv=nv),
        grid=(b,),
        in_specs=[pl.BlockSpec((None, lp, w), lambda bi: (bi, 0, 0)),
                  pl.BlockSpec((1, w), lambda bi: (0, 0)),
                  pl.BlockSpec((1, w), lambda bi: (0, 0))],
        out_specs=pl.BlockSpec((None, lp, w), lambda bi: (bi, 0, 0)),
        out_shape=jax.ShapeDtypeStruct((b, lp, w), F32),
        compiler_params=_params(8 * lp * w * 4, 1),
        name="delta_gates",
    )(ba3, alog, dtb)


def _gate_rows(gcb, nv):
    b, lp, _ = gcb.shape
    nkh = nv // HEADS_PER_KEY
    nc = lp // CHUNK
    gcr = gcb[:, :, nv:].reshape(b, nc, CHUNK, nkh, HEADS_PER_KEY).transpose(0, 3, 1, 4, 2)
    gl = gcr[..., CHUNK - 1:]
    return jnp.concatenate(
        [gcr.reshape(b, nkh, nc, PAIR_W),
         jnp.broadcast_to(gl, (b, nkh, nc, HEADS_PER_KEY, CHUNK)).reshape(b, nkh, nc, PAIR_W),
         jnp.broadcast_to(gl, (b, nkh, nc, HEADS_PER_KEY, DV)).reshape(b, nkh, nc, HEADS_PER_KEY * DV)], axis=-1)


def _block_diag(a, b):
    z = jnp.zeros_like(a)
    return jnp.concatenate([jnp.concatenate([a, z], axis=1), jnp.concatenate([z, b], axis=1)], axis=0)


def _delta_kernel(q_ref, k_ref, v_ref, z_ref, cwq_ref, cwk_ref, cwv_ref, gcb_ref, rows_ref, onorm_ref,
                  o_ref, u_s, wq_s, qk_s, kb_s, edg_s, st_s, *, nv, keys, group):
    kp = pl.program_id(1)
    n_chunks = q_ref.shape[0] // CHUNK
    vw = HEADS_PER_KEY * DV
    rowi = lax.broadcasted_iota(jnp.int32, (CHUNK, PAIR_W), 0)
    lanei = lax.broadcasted_iota(jnp.int32, (CHUNK, PAIR_W), 1)
    coli = lanei & (CHUNK - 1)
    left = lanei < CHUNK
    lower = rowi >= coli
    strict = rowi > coli
    eye = jnp.where(rowi == coli, 1.0, 0.0).astype(F32)
    lane_g = lax.broadcasted_iota(jnp.int32, (CHUNK, 2 * nv), 1)
    nt_dims = (((1,), (1,)), ((), ()))
    tn_dims = (((0,), (0,)), ((), ()))

    def mm_pair(x, y):
        ybd = jnp.concatenate([jnp.where(left, y, 0.0), jnp.where(left, 0.0, y)], axis=0).astype(BF16)
        return jnp.dot(x.astype(BF16), ybd, preferred_element_type=F32)

    def l2n(t):
        return t * lax.rsqrt(jnp.sum(t * t, axis=-1, keepdims=True) + EPS)

    def conv_silu(ref, cw_ref, rows, lb_rows, c0, width):
        x = jnp.concatenate([ref[lb_rows, c0:c0 + width], ref[rows, c0:c0 + width]], axis=0).astype(F32)
        w = cw_ref[:, c0:c0 + width]
        y = x[LOOKBACK:] * w[CONV_K - 1:CONV_K]
        for j in range(1, CONV_K):
            y = y + x[LOOKBACK - j:LOOKBACK - j + CHUNK] * w[CONV_K - 1 - j:CONV_K - j]
        return y * jax.nn.sigmoid(y)

    def local_stages(g):
        chains = []
        for ci in range(group):
            c = g * group + ci
            r0 = pl.multiple_of(c * CHUNK, CHUNK)
            lb_rows = pl.ds(pl.multiple_of(jnp.maximum(r0 - LOOKBACK, 0), LOOKBACK), LOOKBACK)
            for e in range(keys):
                chains.append(dict(c=c, e=e, rows=pl.ds(r0, CHUNK), lb_rows=lb_rows))

        for ch in chains:
            e, rows, lb_rows = ch["e"], ch["rows"], ch["lb_rows"]
            q = l2n(conv_silu(q_ref, cwq_ref, rows, lb_rows, e * DK, DK)) * (DK ** -0.5)
            k = l2n(conv_silu(k_ref, cwk_ref, rows, lb_rows, e * DK, DK))
            qb = q.astype(BF16)
            kb = k.astype(BF16)
            qkk = lax.dot_general(jnp.concatenate([qb, kb], axis=0), jnp.concatenate([kb, kb], axis=0),
                                  nt_dims, preferred_element_type=F32)
            ch.update(q=q, k=k, kb=kb, qkk=qkk)
        yield

        for ch in chains:
            gcb = gcb_ref[ch["rows"], :]

            def column(idx, gcb=gcb):
                return jnp.sum(jnp.where(lane_g == idx, gcb, 0.0), axis=1, keepdims=True)

            h0 = HEADS_PER_KEY * (keys * kp + ch["e"])
            beta_a, beta_b = column(h0), column(h0 + 1)
            gc_a, gc_b = column(nv + h0), column(nv + h0 + 1)
            gc = jnp.where(left, gc_a, gc_b)
            rr = rows_ref[ch["e"], pl.ds(ch["c"], 1), :]
            decay = jnp.where(lower, jnp.exp(jnp.where(lower, gc - rr[:, :PAIR_W], 0.0)), 0.0)
            n = -jnp.where(strict, (jnp.where(left, beta_a, beta_b) * ch["qkk"][CHUNK:]) * decay, 0.0)
            ch.update(beta_a=beta_a, beta_b=beta_b, eg_a=jnp.exp(gc_a), eg_b=jnp.exp(gc_b), decay=decay,
                      edg=jnp.exp(rr[:, PAIR_W:2 * PAIR_W] - gc), t=eye + n, p=n)

        s = 2
        while s < CHUNK:
            for ch in chains:
                ch["p"] = mm_pair(ch["p"], ch["p"])
            yield
            for ch in chains:
                ch["t"] = ch["t"] + mm_pair(ch["p"], ch["t"])
            yield
            s *= 2

        for ch in chains:
            e, rows = ch["e"], ch["rows"]
            v = conv_silu(v_ref, cwv_ref, rows, ch["lb_rows"], e * vw, vw)
            k = ch["k"]
            rhs_a = jnp.concatenate([v[:, :DV] * ch["beta_a"], k * (ch["beta_a"] * ch["eg_a"])], axis=1)
            rhs_b = jnp.concatenate([v[:, DV:] * ch["beta_b"], k * (ch["beta_b"] * ch["eg_b"])], axis=1)
            ch["uw"] = jnp.dot(ch["t"].astype(BF16), _block_diag(rhs_a.astype(BF16), rhs_b.astype(BF16)),
                               preferred_element_type=F32)
        yield

        for ch in chains:
            e, rows, uw, q = ch["e"], ch["rows"], ch["uw"], ch["q"]
            wq0 = pl.multiple_of(ch["c"] * (2 * CHUNK), 2 * CHUNK)
            u_s[e, rows, :] = jnp.concatenate([uw[:, :DV], uw[:, 2 * DV:3 * DV]], axis=1)
            wq_s[e, pl.ds(wq0, CHUNK), :] = jnp.concatenate([uw[:, DV:2 * DV], uw[:, 3 * DV:]], axis=1).astype(BF16)
            wq_s[e, pl.ds(wq0 + CHUNK, CHUNK), :] = jnp.concatenate([q * ch["eg_a"], q * ch["eg_b"]],
                                                                     axis=1).astype(BF16)
            qk_s[e, rows, :] = (ch["qkk"][:CHUNK] * ch["decay"]).astype(BF16)
            kb_s[e, rows, :] = ch["kb"]
            edg_s[e, rows, :] = ch["edg"]

    onorm = onorm_ref[...]

    def scan_stages(g):
        states = [st_s[e] for e in range(keys)]
        for ci in range(group):
            c = g * group + ci
            rows = pl.ds(pl.multiple_of(c * CHUNK, CHUNK), CHUNK)
            wq_rows = pl.ds(pl.multiple_of(c * (2 * CHUNK), 2 * CHUNK), 2 * CHUNK)
            r1s, v_news, outs = [], [], []
            for e in range(keys):
                sb = states[e].astype(BF16)
                r1s.append(jnp.dot(wq_s[e, wq_rows, :], _block_diag(sb[:, :DV], sb[:, DV:]),
                                   preferred_element_type=F32))
            yield
            for e in range(keys):
                v_news.append(u_s[e, rows, :] - r1s[e][:CHUNK])
            for e in range(keys):
                v_new = v_news[e]
                edg = edg_s[e, rows, :]
                edg_sw = pltpu.roll(edg, CHUNK, 1)
                dv = jnp.concatenate([v_new[:, :DV] * jnp.where(left, edg, edg_sw),
                                      v_new[:, DV:] * jnp.where(left, edg_sw, edg)], axis=1).astype(BF16)
                egl = jnp.exp(rows_ref[e, pl.ds(c, 1), 2 * PAIR_W:])
                states[e] = states[e] * egl + lax.dot_general(kb_s[e, rows, :], dv, tn_dims,
                                                              preferred_element_type=F32)
            for e in range(keys):
                vb = v_news[e].astype(BF16)
                outs.append(r1s[e][CHUNK:] + jnp.dot(qk_s[e, rows, :], _block_diag(vb[:, :DV], vb[:, DV:]),
                                                     preferred_element_type=F32))
            yield
            for e in range(keys):
                for j in range(HEADS_PER_KEY):
                    oj = outs[e][:, j * DV:(j + 1) * DV]
                    on = oj * lax.rsqrt(jnp.mean(oj * oj, axis=-1, keepdims=True) + EPS) * onorm
                    c0 = e * vw + j * DV
                    z = z_ref[rows, c0:c0 + DV].astype(F32)
                    o_ref[rows, c0:c0 + DV] = (on * (z * jax.nn.sigmoid(z))).astype(o_ref.dtype)
        for e in range(keys):
            st_s[e] = states[e]

    def run(local_gen, scan_gen):
        local_live = local_gen is not None
        scan_live = scan_gen is not None
        while local_live or scan_live:
            for _ in range(2):
                if local_live:
                    local_live = next(local_gen, "done") != "done"
            if scan_live:
                scan_live = next(scan_gen, "done") != "done"

    n_groups = n_chunks // group
    st_s[...] = jnp.zeros_like(st_s)
    run(local_stages(0), None)

    def merged(g, carry):
        run(local_stages(g + 1), scan_stages(g))
        return carry

    lax.fori_loop(0, n_groups - 1, merged, 0)
    run(None, scan_stages(n_groups - 1))


def _delta_rule(proj3, conv_w, layer, gcb, rows, out_norm, *, nk_heads, nv, keys=2, group=3):
    b, lp, _ = proj3.shape
    assert nv == HEADS_PER_KEY * nk_heads and nk_heads % keys == 0 and PAIR_W == 128
    assert (lp // CHUNK) % group == 0
    kw = keys * DK
    vw = keys * HEADS_PER_KEY * DV
    key_dim = nk_heads * DK
    k_blk = key_dim // kw
    v_blk = 2 * key_dim // vw
    z_blk = (2 * key_dim + nv * DV) // vw
    n_chunks = lp // CHUNK
    pvw = HEADS_PER_KEY * DV
    scratch = [pltpu.VMEM((keys, lp, pvw), F32),
               pltpu.VMEM((keys, 2 * lp, pvw), BF16),
               pltpu.VMEM((keys, lp, PAIR_W), BF16),
               pltpu.VMEM((keys, lp, DK), BF16),
               pltpu.VMEM((keys, lp, PAIR_W), F32),
               pltpu.VMEM((keys, DK, pvw), F32)]
    vmem = (2 * lp * (2 * kw + 3 * vw) * 2 + 2 * lp * 2 * nv * 4
            + keys * lp * (pvw * 4 + 2 * pvw * 2 + PAIR_W * 2 + DK * 2 + PAIR_W * 4) + 8 * MIB)
    return pl.pallas_call(
        functools.partial(_delta_kernel, nv=nv, keys=keys, group=group),
        grid=(b, nk_heads // keys),
        in_specs=[pl.BlockSpec((None, lp, kw), lambda bi, kp: (bi, 0, kp)),
                  pl.BlockSpec((None, lp, kw), lambda bi, kp: (bi, 0, k_blk + kp)),
                  pl.BlockSpec((None, lp, vw), lambda bi, kp: (bi, 0, v_blk + kp)),
                  pl.BlockSpec((None, lp, vw), lambda bi, kp: (bi, 0, z_blk + kp)),
                  pl.BlockSpec((None, CONV_K, kw), lambda bi, kp: (layer, 0, kp)),
                  pl.BlockSpec((None, CONV_K, kw), lambda bi, kp: (layer, 0, k_blk + kp)),
                  pl.BlockSpec((None, CONV_K, vw), lambda bi, kp: (layer, 0, v_blk + kp)),
                  pl.BlockSpec((None, lp, 2 * nv), lambda bi, kp: (bi, 0, 0)),
                  pl.BlockSpec((None, keys, n_chunks, rows.shape[-1]), lambda bi, kp: (bi, kp, 0, 0)),
                  pl.BlockSpec((1, DV), lambda bi, kp: (0, 0))],
        out_specs=pl.BlockSpec((None, lp, vw), lambda bi, kp: (bi, 0, kp)),
        out_shape=jax.ShapeDtypeStruct((b, lp, nv * DV), BF16),
        scratch_shapes=scratch,
        compiler_params=_params(vmem, 2),
        name="delta_rule",
    )(proj3, proj3, proj3, proj3, conv_w, conv_w, conv_w, gcb, rows, out_norm.reshape(1, DV))


def _pool_kernel(lb_ref, x_ref, g_ref, o_ref, *, col_blk):
    i = pl.program_id(1)
    tr, d = x_ref.shape
    gw = d // len(POOL_WINDOWS)
    x = x_ref[...]
    lb = lb_ref[...]
    inv = lax.rsqrt(jnp.mean(x * x, axis=-1, keepdims=True) + EPS)
    inv_lb = lax.rsqrt(jnp.mean(lb * lb, axis=-1, keepdims=True) + EPS)
    row = i * tr - LOOKBACK + lax.broadcasted_iota(jnp.int32, (tr + LOOKBACK, 1), 0)
    valid = row >= PAD
    pos = (row[LOOKBACK:] - (PAD - 1)).astype(F32)
    for gi, win in enumerate(POOL_WINDOWS):
        div = jnp.where(valid[LOOKBACK:], jnp.minimum(pos, float(win)), 1.0)
        for cb in range(gw // col_blk):
            c0 = gi * gw + cb * col_blk
            gv = g_ref[:, c0:c0 + col_blk]
            xe = jnp.concatenate([lb[:, c0:c0 + col_blk] * inv_lb * gv, x[:, c0:c0 + col_blk] * inv * gv], axis=0)
            xe = jnp.where(valid, xe, 0.0)
            s = xe
            sh = 1
            while sh < win:
                s = s + pltpu.roll(s, sh, 0)
                sh *= 2
            o_ref[:, c0:c0 + col_blk] = (s[LOOKBACK:] / div - xe[LOOKBACK:]).astype(o_ref.dtype)


def _pool(h3, g):
    b, lp, d = h3.shape
    tr = CHUNK
    per = tr // LOOKBACK
    return pl.pallas_call(
        functools.partial(_pool_kernel, col_blk=512),
        grid=(b, lp // tr),
        in_specs=[pl.BlockSpec((None, LOOKBACK, d), lambda bi, i: (bi, jnp.maximum(i * per - 1, 0), 0)),
                  pl.BlockSpec((None, tr, d), lambda bi, i: (bi, i, 0)),
                  pl.BlockSpec((1, d), lambda bi, i: (0, 0))],
        out_specs=pl.BlockSpec((None, tr, d), lambda bi, i: (bi, i, 0)),
        out_shape=jax.ShapeDtypeStruct((b, lp, d), BF16),
        compiler_params=_params(8 * tr * d * 4, 2),
        name="pool",
    )(h3, h3, g.reshape(1, d))


def kernel(x, meta_tokens, mix_norm, dn_w_in, dn_conv_w, dn_a_log, dn_dt_bias, dn_out_norm, dn_w_out, pool_w,
           pool_scale, mlp_norm, w_up, w_down, final_norm):
    b, seq, d = x.shape
    nv = dn_a_log.shape[1]
    val_dim = nv * DV
    key_dim = (dn_conv_w.shape[2] - val_dim) // 2
    nk_heads = key_dim // DK
    qkvz = 2 * key_dim + 2 * val_dim
    d_ff = w_up.shape[2]
    depth = mix_norm.shape[0]
    lp = PAD + N_META + seq
    m = b * lp
    tm = m // 8
    tm_stream = m // 6
    assert lp % CHUNK == 0 and tm % 16 == 0 and tm_stream % 16 == 0 and dn_w_in.shape[2] == qkvz + 2 * nv

    meta = jnp.broadcast_to(meta_tokens.astype(x.dtype)[None], (b, N_META, d))
    h = jnp.concatenate([jnp.zeros((b, PAD, d), x.dtype), meta, x], axis=1).reshape(m, d)

    for i in range(depth):
        j = i // 2
        if i % 2 == 0:
            hn = _rmsnorm(h, mix_norm[i], BF16)
            proj = _matmul_stream(hn, dn_w_in, j, n=qkvz, tn=512, tm=tm_stream, out_dtype=BF16)
            ba = _matmul(hn, dn_w_in, j, n=2 * nv, n_off=qkvz, tn=2 * nv, tk=d, tm=tm, out_dtype=F32)
            gcb = _gates(ba.reshape(b, lp, 2 * nv), dn_a_log[j], dn_dt_bias[j])
            og = _delta_rule(proj.reshape(b, lp, qkvz), dn_conv_w, j, gcb, _gate_rows(gcb, nv), dn_out_norm[j],
                             nk_heads=nk_heads, nv=nv)
            h = _matmul(og.reshape(m, val_dim), dn_w_out, j, n=d, tn=512, tk=2048, tm=tm, out_dtype=F32, res=h)
        else:
            pooled = _pool(h.reshape(b, lp, d), mix_norm[i])
            h = _pool_matmul(pooled.reshape(m, d), pool_w, j, pool_scale[j], h, tm=tm)
        hn = _rmsnorm(h, mlp_norm[i], BF16)
        act = _matmul_stream(hn, w_up, i, n=d_ff, tn=512, tm=tm_stream, out_dtype=BF16, relu2=True)
        h = _matmul(act, w_down, i, n=d, tn=512, tk=2048, tm=tm, out_dtype=F32, res=h)

    return _final_norm(h.reshape(b, lp, d), final_norm, seq)
```

```python
import functools

import jax
import jax.numpy as jnp
from jax import lax
from jax.experimental import pallas as pl
from jax.experimental.pallas import tpu as pltpu

F32 = jnp.float32
BF16 = jnp.bfloat16

N_META = 16
CHUNK = 64
DK = 128
DV = 128
CONV_K = 4
POOL_WINDOWS = (2, 4, 8, 16)
EPS = 1e-6
PAD = (-N_META) % CHUNK
LOOKBACK = 16
HEADS_PER_KEY = 2
PAIR_W = HEADS_PER_KEY * CHUNK

V7X_VMEM_BYTES = 64 * 1024 * 1024
MIB = 1024 * 1024


def _params(vmem_bytes, n_axes):
    limit = min(int(vmem_bytes) + 4 * MIB, V7X_VMEM_BYTES - 6 * MIB)
    return pltpu.CompilerParams(dimension_semantics=("arbitrary",) * n_axes, vmem_limit_bytes=limit)


def _rmsnorm_kernel(x_ref, g_ref, o_ref):
    x = x_ref[...]
    ms = jnp.mean(x * x, axis=-1, keepdims=True)
    o_ref[...] = (x * lax.rsqrt(ms + EPS) * g_ref[...]).astype(o_ref.dtype)


def _rmsnorm(x, g, out_dtype, tr=256):
    m, d = x.shape
    return pl.pallas_call(
        _rmsnorm_kernel,
        grid=(m // tr,),
        in_specs=[pl.BlockSpec((tr, d), lambda i: (i, 0)), pl.BlockSpec((1, d), lambda i: (0, 0))],
        out_specs=pl.BlockSpec((tr, d), lambda i: (i, 0)),
        out_shape=jax.ShapeDtypeStruct((m, d), out_dtype),
        compiler_params=_params(6 * tr * d * 4, 1),
        name="rmsnorm",
    )(x, g.reshape(1, d))


def _final_norm(h3, g, seq):
    b, lp, d = h3.shape
    tr = CHUNK
    skip = (lp - seq) // tr
    return pl.pallas_call(
        _rmsnorm_kernel,
        grid=(b, seq // tr),
        in_specs=[pl.BlockSpec((None, tr, d), lambda bi, i: (bi, i + skip, 0)),
                  pl.BlockSpec((1, d), lambda bi, i: (0, 0))],
        out_specs=pl.BlockSpec((None, tr, d), lambda bi, i: (bi, i, 0)),
        out_shape=jax.ShapeDtypeStruct((b, seq, d), h3.dtype),
        compiler_params=_params(6 * tr * d * 4, 2),
        name="final_norm",
    )(h3, g.reshape(1, d))


def _mm_kernel(*refs, nk, tm, relu2, has_res, has_scale):
    it = iter(refs)
    a_ref = next(it)
    w_ref = next(it)
    res_ref = next(it) if has_res else None
    scale_ref = next(it) if has_scale else None
    o_ref = next(it)
    wb_ref = next(it)
    acc_ref = next(it) if nk > 1 else None
    k = pl.program_id(1)
    i = pl.program_id(2)

    @pl.when(i == 0)
    def _cast_weights():
        wb_ref[...] = w_ref[...].astype(BF16)

    def product():
        return jnp.dot(a_ref[...], wb_ref[...], preferred_element_type=F32)

    def finish(y, add_res):
        if relu2:
            r = jnp.maximum(y, 0.0)
            y = r * r
        if has_scale:
            y = y * scale_ref[...]
        if add_res:
            y = res_ref[...] + y
        o_ref[...] = y.astype(o_ref.dtype)

    if nk == 1:
        finish(product(), has_res)
    else:
        assert not (relu2 or has_scale)
        rows = pl.ds(pl.multiple_of(i * tm, 8), tm)

        @pl.when(k == 0)
        def _init():
            acc_ref[rows, :] = res_ref[...] + product() if has_res else product()

        @pl.when(jnp.logical_and(k > 0, k < nk - 1))
        def _accumulate():
            acc_ref[rows, :] += product()

        @pl.when(k == nk - 1)
        def _finish():
            finish(acc_ref[rows, :] + product(), False)


def _matmul(a, w, layer, *, n, n_off=0, tn, tk, tm, out_dtype, relu2=False, res=None):
    m, kdim = a.shape
    nk = kdim // tk
    assert m % tm == 0 and kdim % tk == 0 and n % tn == 0 and n_off % tn == 0
    joff = n_off // tn
    last = nk - 1

    def o_map(j, k, i):
        return (jnp.where(k == last, i, 0), j) if nk > 1 else (i, j)

    in_specs = [pl.BlockSpec((tm, tk), lambda j, k, i: (i, k)),
                pl.BlockSpec((None, tk, tn), lambda j, k, i: (layer, k, j + joff))]
    args = [a, w]
    if res is not None:
        n_i = m // tm
        in_specs.append(pl.BlockSpec((tm, tn), (lambda j, k, i: (jnp.where(k == 0, i, n_i - 1), j))
                                     if nk > 1 else o_map))
        args.append(res)
    scratch = [pltpu.VMEM((tk, tn), BF16)]
    if nk > 1:
        scratch.append(pltpu.VMEM((m, tn), F32))
    osz = jnp.dtype(out_dtype).itemsize
    vmem = (2 * tk * tn * 4 + tk * tn * 2 + 2 * tm * tk * 2 + 2 * tm * tn * osz + 2 * tm * tn * 4
            + (2 * tm * tn * 4 if res is not None else 0) + (m * tn * 4 if nk > 1 else 0))
    return pl.pallas_call(
        functools.partial(_mm_kernel, nk=nk, tm=tm, relu2=relu2, has_res=res is not None, has_scale=False),
        grid=(n // tn, nk, m // tm),
        in_specs=in_specs,
        out_specs=pl.BlockSpec((tm, tn), o_map),
        out_shape=jax.ShapeDtypeStruct((m, n), out_dtype),
        scratch_shapes=scratch,
        compiler_params=_params(vmem, 3),
        name="matmul",
    )(*args)


def _mm_stream_kernel(a_ref, w_ref, o_ref, *, relu2):
    y = jnp.dot(a_ref[...], w_ref[...].astype(BF16), preferred_element_type=F32)
    if relu2:
        r = jnp.maximum(y, 0.0)
        y = r * r
    o_ref[...] = y.astype(o_ref.dtype)


def _matmul_stream(a, w, layer, *, n, tn, tm, out_dtype, relu2=False):
    m, kdim = a.shape
    assert m % tm == 0 and n % tn == 0
    osz = jnp.dtype(out_dtype).itemsize
    vmem = 2 * kdim * tn * 4 + kdim * tn * 2 + tm * kdim * 2 + 2 * tm * tn * osz + 2 * tm * tn * 4
    return pl.pallas_call(
        functools.partial(_mm_stream_kernel, relu2=relu2),
        grid=(m // tm, n // tn),
        in_specs=[pl.BlockSpec((tm, kdim), lambda i, j: (i, 0), pipeline_mode=pl.Buffered(1)),
                  pl.BlockSpec((None, kdim, tn), lambda i, j: (layer, 0, j))],
        out_specs=pl.BlockSpec((tm, tn), lambda i, j: (i, j)),
        out_shape=jax.ShapeDtypeStruct((m, n), out_dtype),
        compiler_params=_params(vmem, 2),
        name="matmul_stream",
    )(a, w)


def _pool_matmul(pooled, pool_w, layer, scale, res, *, tm):
    m, d = pooled.shape
    n_groups, gw = pool_w.shape[1], pool_w.shape[2]
    vmem = 2 * gw * gw * 4 + gw * gw * 2 + 2 * tm * gw * 2 + 6 * tm * gw * 4
    return pl.pallas_call(
        functools.partial(_mm_kernel, nk=1, tm=tm, relu2=False, has_res=True, has_scale=True),
        grid=(n_groups, 1, m // tm),
        in_specs=[pl.BlockSpec((tm, gw), lambda j, k, i: (i, j)),
                  pl.BlockSpec((None, None, gw, gw), lambda j, k, i: (layer, j, 0, 0)),
                  pl.BlockSpec((tm, gw), lambda j, k, i: (i, j)),
                  pl.BlockSpec((1, gw), lambda j, k, i: (0, j))],
        out_specs=pl.BlockSpec((tm, gw), lambda j, k, i: (i, j)),
        out_shape=jax.ShapeDtypeStruct((m, d), res.dtype),
        scratch_shapes=[pltpu.VMEM((gw, gw), BF16)],
        compiler_params=_params(vmem, 3),
        name="pool_matmul",
    )(pooled, pool_w, res, scale.reshape(1, d))


def _gate_kernel(ba_ref, alog_ref, dtb_ref, o_ref, *, nv):
    n_chunks = ba_ref.shape[0] // CHUNK
    lane = lax.broadcasted_iota(jnp.int32, (CHUNK, 2 * nv), 1)
    pos = lax.broadcasted_iota(jnp.int32, (CHUNK, 2 * nv), 0)
    neg_a = -jnp.exp(alog_ref[...])
    dtb = dtb_ref[...]

    def body(c, carry):
        rows = pl.ds(pl.multiple_of(c * CHUNK, CHUNK), CHUNK)
        x = ba_ref[rows, :]
        beta = jax.nn.sigmoid(x)
        xa = x + dtb
        g = neg_a * (jnp.maximum(xa, 0.0) + jnp.log1p(jnp.exp(-jnp.abs(xa))))
        val = jnp.where(lane < nv, beta, g)
        val = jnp.where(jnp.logical_or(c > 0, pos >= PAD), val, 0.0)
        cs = val
        s = 1
        while s < CHUNK:
            cs = cs + jnp.where(pos >= s, pltpu.roll(cs, s, 0), 0.0)
            s *= 2
        o_ref[rows, :] = jnp.where(lane < nv, val, cs)
        return carry

    lax.fori_loop(0, n_chunks, body, 0)


def _gates(ba3, a_log, dt_bias):
    b, lp, w = ba3.shape
    nv = w // 2
    zeros = jnp.zeros((nv,), F32)
    alog = jnp.concatenate([zeros, a_log.astype(F32)]).reshape(1, w)
    dtb = jnp.concatenate([zeros, dt_bias.astype(F32)]).reshape(1, w)
    return pl.pallas_call(
        functools.partial(_gate_kernel, nv=nv),
        grid=(b,),
        in_specs=[pl.BlockSpec((None, lp, w), lambda bi: (bi, 0, 0)),
                  pl.BlockSpec((1, w), lambda bi: (0, 0)),
                  pl.BlockSpec((1, w), lambda bi: (0, 0))],
        out_specs=pl.BlockSpec((None, lp, w), lambda bi: (bi, 0, 0)),
        out_shape=jax.ShapeDtypeStruct((b, lp, w), F32),
        compiler_params=_params(8 * lp * w * 4, 1),
        name="delta_gates",
    )(ba3, alog, dtb)


def _gate_rows(gcb, nv):
    b, lp, _ = gcb.shape
    nkh = nv // HEADS_PER_KEY
    nc = lp // CHUNK
    gcr = gcb[:, :, nv:].reshape(b, nc, CHUNK, nkh, HEADS_PER_KEY).transpose(0, 3, 1, 4, 2)
    gl = gcr[..., CHUNK - 1:]
    return jnp.concatenate(
        [gcr.reshape(b, nkh, nc, PAIR_W),
         jnp.broadcast_to(gl, (b, nkh, nc, HEADS_PER_KEY, CHUNK)).reshape(b, nkh, nc, PAIR_W),
         jnp.broadcast_to(gl, (b, nkh, nc, HEADS_PER_KEY, DV)).reshape(b, nkh, nc, HEADS_PER_KEY * DV)], axis=-1)


def _block_diag(a, b):
    z = jnp.zeros_like(a)
    return jnp.concatenate([jnp.concatenate([a, z], axis=1), jnp.concatenate([z, b], axis=1)], axis=0)


def _delta_kernel(q_ref, k_ref, v_ref, z_ref, cwq_ref, cwk_ref, cwv_ref, gcb_ref, rows_ref, onorm_ref,
                  o_ref, u_s, wq_s, qk_s, kb_s, edg_s, st_s, *, nv, keys, group):
    kp = pl.program_id(1)
    n_chunks = q_ref.shape[0] // CHUNK
    vw = HEADS_PER_KEY * DV
    rowi = lax.broadcasted_iota(jnp.int32, (CHUNK, PAIR_W), 0)
    lanei = lax.broadcasted_iota(jnp.int32, (CHUNK, PAIR_W), 1)
    coli = lanei & (CHUNK - 1)
    left = lanei < CHUNK
    lower = rowi >= coli
    strict = rowi > coli
    eye = jnp.where(rowi == coli, 1.0, 0.0).astype(F32)
    lane_g = lax.broadcasted_iota(jnp.int32, (CHUNK, 2 * nv), 1)
    nt_dims = (((1,), (1,)), ((), ()))
    tn_dims = (((0,), (0,)), ((), ()))

    def mm_pair(x, y):
        ybd = jnp.concatenate([jnp.where(left, y, 0.0), jnp.where(left, 0.0, y)], axis=0).astype(BF16)
        return jnp.dot(x.astype(BF16), ybd, preferred_element_type=F32)

    def l2n(t, scale=1.0):
        return t * (lax.rsqrt(jnp.sum(t * t, axis=-1, keepdims=True) + EPS) * scale)

    def conv_silu(ref, cw_ref, rows, lb_rows, c0, width):
        x = jnp.concatenate([ref[lb_rows, c0:c0 + width], ref[rows, c0:c0 + width]], axis=0).astype(F32)
        w = cw_ref[:, c0:c0 + width]
        y = x[LOOKBACK:] * w[CONV_K - 1:CONV_K]
        for j in range(1, CONV_K):
            y = y + x[LOOKBACK - j:LOOKBACK - j + CHUNK] * w[CONV_K - 1 - j:CONV_K - j]
        return y * jax.nn.sigmoid(y)

    LOCAL_STAGES = 10

    def local_stages(g):
        chains = []
        for ci in range(group):
            c = g * group + ci
            r0 = pl.multiple_of(c * CHUNK, CHUNK)
            lb_rows = pl.ds(pl.multiple_of(jnp.maximum(r0 - LOOKBACK, 0), LOOKBACK), LOOKBACK)
            for e in range(keys):
                chains.append(dict(c=c, e=e, rows=pl.ds(r0, CHUNK), lb_rows=lb_rows))

        for ch in chains:
            e, rows, lb_rows = ch["e"], ch["rows"], ch["lb_rows"]
            q = l2n(conv_silu(q_ref, cwq_ref, rows, lb_rows, e * DK, DK), DK ** -0.5)
            k = l2n(conv_silu(k_ref, cwk_ref, rows, lb_rows, e * DK, DK))
            qb = q.astype(BF16)
            kb = k.astype(BF16)
            qkk = lax.dot_general(jnp.concatenate([qb, kb], axis=0), jnp.concatenate([kb, kb], axis=0),
                                  nt_dims, preferred_element_type=F32)
            ch.update(q=q, k=k, kb=kb, qkk=qkk)
        yield

        for ch in chains:
            gcb = gcb_ref[ch["rows"], :]

            def column(idx, gcb=gcb):
                return jnp.sum(jnp.where(lane_g == idx, gcb, 0.0), axis=1, keepdims=True)

            h0 = HEADS_PER_KEY * (keys * kp + ch["e"])
            beta_a, beta_b = column(h0), column(h0 + 1)
            gc_a, gc_b = column(nv + h0), column(nv + h0 + 1)
            gc = jnp.where(left, gc_a, gc_b)
            rr = rows_ref[ch["e"], pl.ds(ch["c"], 1), :]
            decay = jnp.where(lower, jnp.exp(jnp.where(lower, gc - rr[:, :PAIR_W], 0.0)), 0.0)
            n = -jnp.where(strict, (jnp.where(left, beta_a, beta_b) * ch["qkk"][CHUNK:]) * decay, 0.0)
            ch.update(beta_a=beta_a, beta_b=beta_b, eg_a=jnp.exp(gc_a), eg_b=jnp.exp(gc_b), decay=decay,
                      edg=jnp.exp(rr[:, PAIR_W:2 * PAIR_W] - gc), t=eye + n, p=n)

        assert CHUNK == 64

        def times_one_plus(x, p):
            return x + mm_pair(p, x)

        for ch in chains:
            ch["p2"] = mm_pair(ch["p"], ch["p"])
        yield
        for ch in chains:
            ch["p4"] = mm_pair(ch["p2"], ch["p2"])
        for ch in chains:
            ch["a1"] = times_one_plus(ch["t"], ch["p2"])
        yield
        for ch in chains:
            ch["p8"] = mm_pair(ch["p4"], ch["p4"])
        yield
        for ch in chains:
            ch["p16"] = mm_pair(ch["p8"], ch["p8"])
        for ch in chains:
            ch["a2"] = times_one_plus(eye + ch["p4"], ch["p8"])
        yield
        for ch in chains:
            ch["p32"] = mm_pair(ch["p16"], ch["p16"])
        for ch in chains:
            ch["a12"] = mm_pair(ch["a1"], ch["a2"])
        yield
        for ch in chains:
            ch["a3"] = times_one_plus(eye + ch["p16"], ch["p32"])
        yield
        for ch in chains:
            ch["t"] = mm_pair(ch["a12"], ch["a3"])
        yield

        for ch in chains:
            e, rows = ch["e"], ch["rows"]
            v = conv_silu(v_ref, cwv_ref, rows, ch["lb_rows"], e * vw, vw)
            k = ch["k"]
            rhs_a = jnp.concatenate([v[:, :DV] * ch["beta_a"], k * (ch["beta_a"] * ch["eg_a"])], axis=1)
            rhs_b = jnp.concatenate([v[:, DV:] * ch["beta_b"], k * (ch["beta_b"] * ch["eg_b"])], axis=1)
            ch["uw"] = jnp.dot(ch["t"].astype(BF16), _block_diag(rhs_a.astype(BF16), rhs_b.astype(BF16)),
                               preferred_element_type=F32)
        yield

        for ch in chains:
            e, rows, uw, q = ch["e"], ch["rows"], ch["uw"], ch["q"]
            wq0 = pl.multiple_of(ch["c"] * (2 * CHUNK), 2 * CHUNK)
            u_s[e, rows, :] = jnp.concatenate([uw[:, :DV], uw[:, 2 * DV:3 * DV]], axis=1)
            wq_s[e, pl.ds(wq0, CHUNK), :] = jnp.concatenate([uw[:, DV:2 * DV], uw[:, 3 * DV:]], axis=1).astype(BF16)
            wq_s[e, pl.ds(wq0 + CHUNK, CHUNK), :] = jnp.concatenate([q * ch["eg_a"], q * ch["eg_b"]],
                                                                     axis=1).astype(BF16)
            qk_s[e, rows, :] = (ch["qkk"][:CHUNK] * ch["decay"]).astype(BF16)
            kb_s[e, rows, :] = ch["kb"]
            edg_s[e, rows, :] = ch["edg"]

    onorm = onorm_ref[...]

    def scan_stages(g):
        states = [st_s[e] for e in range(keys)]
        for ci in range(group):
            c = g * group + ci
            rows = pl.ds(pl.multiple_of(c * CHUNK, CHUNK), CHUNK)
            wq_rows = pl.ds(pl.multiple_of(c * (2 * CHUNK), 2 * CHUNK), 2 * CHUNK)
            r1s, v_news, outs = [], [], []
            for e in range(keys):
                sb = states[e].astype(BF16)
                r1s.append(jnp.dot(wq_s[e, wq_rows, :], _block_diag(sb[:, :DV], sb[:, DV:]),
                                   preferred_element_type=F32))
            yield
            for e in range(keys):
                v_news.append(u_s[e, rows, :] - r1s[e][:CHUNK])
            for e in range(keys):
                v_new = v_news[e]
                edg = edg_s[e, rows, :]
                edg_sw = pltpu.roll(edg, CHUNK, 1)
                dv = jnp.concatenate([v_new[:, :DV] * jnp.where(left, edg, edg_sw),
                                      v_new[:, DV:] * jnp.where(left, edg_sw, edg)], axis=1).astype(BF16)
                egl = jnp.exp(rows_ref[e, pl.ds(c, 1), 2 * PAIR_W:])
                states[e] = states[e] * egl + lax.dot_general(kb_s[e, rows, :], dv, tn_dims,
                                                              preferred_element_type=F32)
            for e in range(keys):
                vb = v_news[e].astype(BF16)
                outs.append(r1s[e][CHUNK:] + jnp.dot(qk_s[e, rows, :], _block_diag(vb[:, :DV], vb[:, DV:]),
                                                     preferred_element_type=F32))
            yield
            for e in range(keys):
                for j in range(HEADS_PER_KEY):
                    oj = outs[e][:, j * DV:(j + 1) * DV]
                    on = oj * lax.rsqrt(jnp.mean(oj * oj, axis=-1, keepdims=True) + EPS) * onorm
                    c0 = e * vw + j * DV
                    z = z_ref[rows, c0:c0 + DV].astype(F32)
                    o_ref[rows, c0:c0 + DV] = (on * (z * jax.nn.sigmoid(z))).astype(o_ref.dtype)
        for e in range(keys):
            st_s[e] = states[e]

    def run(local_gen, scan_gen):
        gens = [g for g in (local_gen, scan_gen) if g is not None]
        counts = {id(local_gen): LOCAL_STAGES, id(scan_gen): 2 * group + 1}
        order = sorted((float(s + 0.5) / counts[id(g)], n, g) for n, g in enumerate(gens)
                       for s in range(counts[id(g)]))
        for _, _, g in order:
            next(g, None)
        for g in gens:
            assert next(g, "done") == "done"

    n_groups = n_chunks // group
    st_s[...] = jnp.zeros_like(st_s)
    run(local_stages(0), None)

    def merged(g, carry):
        run(local_stages(g + 1), scan_stages(g))
        return carry

    lax.fori_loop(0, n_groups - 1, merged, 0)
    run(None, scan_stages(n_groups - 1))


def _delta_rule(proj3, conv_w, layer, gcb, rows, out_norm, *, nk_heads, nv, keys=2, group=3):
    b, lp, _ = proj3.shape
    assert nv == HEADS_PER_KEY * nk_heads and nk_heads % keys == 0 and PAIR_W == 128
    assert (lp // CHUNK) % group == 0
    kw = keys * DK
    vw = keys * HEADS_PER_KEY * DV
    key_dim = nk_heads * DK
    k_blk = key_dim // kw
    v_blk = 2 * key_dim // vw
    z_blk = (2 * key_dim + nv * DV) // vw
    n_chunks = lp // CHUNK
    pvw = HEADS_PER_KEY * DV
    scratch = [pltpu.VMEM((keys, lp, pvw), F32),
               pltpu.VMEM((keys, 2 * lp, pvw), BF16),
               pltpu.VMEM((keys, lp, PAIR_W), BF16),
               pltpu.VMEM((keys, lp, DK), BF16),
               pltpu.VMEM((keys, lp, PAIR_W), F32),
               pltpu.VMEM((keys, DK, pvw), F32)]
    vmem = (2 * lp * (2 * kw + 3 * vw) * 2 + 2 * lp * 2 * nv * 4
            + keys * lp * (pvw * 4 + 2 * pvw * 2 + PAIR_W * 2 + DK * 2 + PAIR_W * 4) + 8 * MIB)
    return pl.pallas_call(
        functools.partial(_delta_kernel, nv=nv, keys=keys, group=group),
        grid=(b, nk_heads // keys),
        in_specs=[pl.BlockSpec((None, lp, kw), lambda bi, kp: (bi, 0, kp)),
                  pl.BlockSpec((None, lp, kw), lambda bi, kp: (bi, 0, k_blk + kp)),
                  pl.BlockSpec((None, lp, vw), lambda bi, kp: (bi, 0, v_blk + kp)),
                  pl.BlockSpec((None, lp, vw), lambda bi, kp: (bi, 0, z_blk + kp)),
                  pl.BlockSpec((None, CONV_K, kw), lambda bi, kp: (layer, 0, kp)),
                  pl.BlockSpec((None, CONV_K, kw), lambda bi, kp: (layer, 0, k_blk + kp)),
                  pl.BlockSpec((None, CONV_K, vw), lambda bi, kp: (layer, 0, v_blk + kp)),
                  pl.BlockSpec((None, lp, 2 * nv), lambda bi, kp: (bi, 0, 0)),
                  pl.BlockSpec((None, keys, n_chunks, rows.shape[-1]), lambda bi, kp: (bi, kp, 0, 0)),
                  pl.BlockSpec((1, DV), lambda bi, kp: (0, 0))],
        out_specs=pl.BlockSpec((None, lp, vw), lambda bi, kp: (bi, 0, kp)),
        out_shape=jax.ShapeDtypeStruct((b, lp, nv * DV), BF16),
        scratch_shapes=scratch,
        compiler_params=_params(vmem, 2),
        name="delta_rule",
    )(proj3, proj3, proj3, proj3, conv_w, conv_w, conv_w, gcb, rows, out_norm.reshape(1, DV))


def _pool_kernel(lb_ref, x_ref, g_ref, o_ref, *, col_blk):
    i = pl.program_id(1)
    tr, d = x_ref.shape
    gw = d // len(POOL_WINDOWS)
    x = x_ref[...]
    lb = lb_ref[...]
    inv = lax.rsqrt(jnp.mean(x * x, axis=-1, keepdims=True) + EPS)
    inv_lb = lax.rsqrt(jnp.mean(lb * lb, axis=-1, keepdims=True) + EPS)
    row = i * tr - LOOKBACK + lax.broadcasted_iota(jnp.int32, (tr + LOOKBACK, 1), 0)
    valid = row >= PAD
    pos = (row[LOOKBACK:] - (PAD - 1)).astype(F32)
    for gi, win in enumerate(POOL_WINDOWS):
        div = jnp.where(valid[LOOKBACK:], jnp.minimum(pos, float(win)), 1.0)
        for cb in range(gw // col_blk):
            c0 = gi * gw + cb * col_blk
            gv = g_ref[:, c0:c0 + col_blk]
            xe = jnp.concatenate([lb[:, c0:c0 + col_blk] * inv_lb * gv, x[:, c0:c0 + col_blk] * inv * gv], axis=0)
            xe = jnp.where(valid, xe, 0.0)
            s = xe
            sh = 1
            while sh < win:
                s = s + pltpu.roll(s, sh, 0)
                sh *= 2
            o_ref[:, c0:c0 + col_blk] = (s[LOOKBACK:] / div - xe[LOOKBACK:]).astype(o_ref.dtype)


def _pool(h3, g):
    b, lp, d = h3.shape
    tr = CHUNK
    per = tr // LOOKBACK
    return pl.pallas_call(
        functools.partial(_pool_kernel, col_blk=512),
        grid=(b, lp // tr),
        in_specs=[pl.BlockSpec((None, LOOKBACK, d), lambda bi, i: (bi, jnp.maximum(i * per - 1, 0), 0)),
                  pl.BlockSpec((None, tr, d), lambda bi, i: (bi, i, 0)),
                  pl.BlockSpec((1, d), lambda bi, i: (0, 0))],
        out_specs=pl.BlockSpec((None, tr, d), lambda bi, i: (bi, i, 0)),
        out_shape=jax.ShapeDtypeStruct((b, lp, d), BF16),
        compiler_params=_params(8 * tr * d * 4, 2),
        name="pool",
    )(h3, h3, g.reshape(1, d))


def kernel(x, meta_tokens, mix_norm, dn_w_in, dn_conv_w, dn_a_log, dn_dt_bias, dn_out_norm, dn_w_out, pool_w,
           pool_scale, mlp_norm, w_up, w_down, final_norm):
    b, seq, d = x.shape
    nv = dn_a_log.shape[1]
    val_dim = nv * DV
    key_dim = (dn_conv_w.shape[2] - val_dim) // 2
    nk_heads = key_dim // DK
    qkvz = 2 * key_dim + 2 * val_dim
    d_ff = w_up.shape[2]
    depth = mix_norm.shape[0]
    lp = PAD + N_META + seq
    m = b * lp
    tm = m // 8
    tm_stream = m // 4
    assert lp % CHUNK == 0 and tm % 16 == 0 and tm_stream % 16 == 0 and dn_w_in.shape[2] == qkvz + 2 * nv

    meta = jnp.broadcast_to(meta_tokens.astype(x.dtype)[None], (b, N_META, d))
    h = jnp.concatenate([jnp.zeros((b, PAD, d), x.dtype), meta, x], axis=1).reshape(m, d)

    for i in range(depth):
        j = i // 2
        if i % 2 == 0:
            hn = _rmsnorm(h, mix_norm[i], BF16)
            proj = _matmul_stream(hn, dn_w_in, j, n=qkvz, tn=512, tm=tm_stream, out_dtype=BF16)
            ba = _matmul(hn, dn_w_in, j, n=2 * nv, n_off=qkvz, tn=2 * nv, tk=d, tm=tm, out_dtype=F32)
            gcb = _gates(ba.reshape(b, lp, 2 * nv), dn_a_log[j], dn_dt_bias[j])
            og = _delta_rule(proj.reshape(b, lp, qkvz), dn_conv_w, j, gcb, _gate_rows(gcb, nv), dn_out_norm[j],
                             nk_heads=nk_heads, nv=nv)
            h = _matmul(og.reshape(m, val_dim), dn_w_out, j, n=d, tn=512, tk=2048, tm=tm, out_dtype=F32, res=h)
        else:
            pooled = _pool(h.reshape(b, lp, d), mix_norm[i])
            h = _pool_matmul(pooled.reshape(m, d), pool_w, j, pool_scale[j], h, tm=tm)
        hn = _rmsnorm(h, mlp_norm[i], BF16)
        act = _matmul_stream(hn, w_up, i, n=d_ff, tn=512, tm=tm_stream, out_dtype=BF16, relu2=True)
        h = _matmul(act, w_down, i, n=d, tn=512, tk=2048, tm=tm, out_dtype=F32, res=h)

    return _final_norm(h.reshape(b, lp, d), final_norm, seq)
```

```python
import functools

import jax
import jax.numpy as jnp
from jax import lax
from jax.experimental import pallas as pl
from jax.experimental.pallas import tpu as pltpu

F32 = jnp.float32
BF16 = jnp.bfloat16

N_META = 16
CHUNK = 64
DK = 128
DV = 128
CONV_K = 4
POOL_WINDOWS = (2, 4, 8, 16)
EPS = 1e-6
PAD = (-N_META) % CHUNK
LOOKBACK = 16
HEADS_PER_KEY = 2
PAIR_W = HEADS_PER_KEY * CHUNK

V7X_VMEM_BYTES = 64 * 1024 * 1024
MIB = 1024 * 1024


def _params(vmem_bytes, n_axes):
    limit = min(int(vmem_bytes) + 4 * MIB, V7X_VMEM_BYTES - 6 * MIB)
    return pltpu.CompilerParams(dimension_semantics=("arbitrary",) * n_axes, vmem_limit_bytes=limit)


def _rmsnorm_kernel(x_ref, g_ref, o_ref):
    x = x_ref[...]
    ms = jnp.mean(x * x, axis=-1, keepdims=True)
    o_ref[...] = (x * lax.rsqrt(ms + EPS) * g_ref[...]).astype(o_ref.dtype)


def _rmsnorm(x, g, out_dtype, tr=256):
    m, d = x.shape
    return pl.pallas_call(
        _rmsnorm_kernel,
        grid=(m // tr,),
        in_specs=[pl.BlockSpec((tr, d), lambda i: (i, 0)), pl.BlockSpec((1, d), lambda i: (0, 0))],
        out_specs=pl.BlockSpec((tr, d), lambda i: (i, 0)),
        out_shape=jax.ShapeDtypeStruct((m, d), out_dtype),
        compiler_params=_params(6 * tr * d * 4, 1),
        name="rmsnorm",
    )(x, g.reshape(1, d))


def _final_norm(h3, g, seq):
    b, lp, d = h3.shape
    tr = CHUNK
    skip = (lp - seq) // tr
    return pl.pallas_call(
        _rmsnorm_kernel,
        grid=(b, seq // tr),
        in_specs=[pl.BlockSpec((None, tr, d), lambda bi, i: (bi, i + skip, 0)),
                  pl.BlockSpec((1, d), lambda bi, i: (0, 0))],
        out_specs=pl.BlockSpec((None, tr, d), lambda bi, i: (bi, i, 0)),
        out_shape=jax.ShapeDtypeStruct((b, seq, d), h3.dtype),
        compiler_params=_params(6 * tr * d * 4, 2),
        name="final_norm",
    )(h3, g.reshape(1, d))


def _pool_mm_kernel(a_ref, w_ref, res_ref, scale_ref, o_ref, wb_ref):
    @pl.when(pl.program_id(1) == 0)
    def _cast_weights():
        wb_ref[...] = w_ref[...].astype(BF16)

    y = jnp.dot(a_ref[...], wb_ref[...], preferred_element_type=F32)
    o_ref[...] = (res_ref[...] + y * scale_ref[...]).astype(o_ref.dtype)


def _mm_stream_kernel(*refs, relu2, has_prev):
    if has_prev:
        a_ref, w_ref, prev_ref, o_ref = refs
    else:
        a_ref, w_ref, o_ref = refs
    y = jnp.dot(a_ref[...], w_ref[...].astype(BF16), preferred_element_type=F32)
    if relu2:
        r = jnp.maximum(y, 0.0)
        y = r * r
    if has_prev:
        y = prev_ref[...] + y
    o_ref[...] = y.astype(o_ref.dtype)


def _matmul_stream(a, w, layer, *, n, tn, tm, out_dtype, n_off=0, relu2=False, prev=None, kc=None, k_blk=0):
    m, kdim = a.shape
    kc = kdim if kc is None else kc
    assert m % tm == 0 and n % tn == 0 and n_off % tn == 0 and kdim % kc == 0
    joff = n_off // tn
    osz = jnp.dtype(out_dtype).itemsize
    vmem = (2 * kc * tn * 4 + kc * tn * 2 + tm * kc * 2 + 2 * tm * tn * osz + tm * tn * 4
            + (2 * tm * tn * 4 if prev is not None else 0))
    in_specs = [pl.BlockSpec((tm, kc), lambda i, j: (i, k_blk), pipeline_mode=pl.Buffered(1)),
                pl.BlockSpec((None, kc, tn), lambda i, j: (layer, k_blk, j + joff))]
    args = [a, w]
    if prev is not None:
        in_specs.append(pl.BlockSpec((tm, tn), lambda i, j: (i, j)))
        args.append(prev)
    return pl.pallas_call(
        functools.partial(_mm_stream_kernel, relu2=relu2, has_prev=prev is not None),
        grid=(m // tm, n // tn),
        in_specs=in_specs,
        out_specs=pl.BlockSpec((tm, tn), lambda i, j: (i, j)),
        out_shape=jax.ShapeDtypeStruct((m, n), out_dtype),
        compiler_params=_params(vmem, 2),
        name="matmul_stream",
    )(*args)


def _matmul_chunked(a, w, layer, res, *, n, tn, tm, kc):
    for k_blk in range(a.shape[1] // kc):
        res = _matmul_stream(a, w, layer, n=n, tn=tn, tm=tm, out_dtype=res.dtype, prev=res, kc=kc, k_blk=k_blk)
    return res


def _pool_matmul(pooled, pool_w, layer, scale, res, *, tm):
    m, d = pooled.shape
    n_groups, gw = pool_w.shape[1], pool_w.shape[2]
    vmem = 2 * gw * gw * 4 + gw * gw * 2 + 2 * tm * gw * 2 + 6 * tm * gw * 4
    return pl.pallas_call(
        _pool_mm_kernel,
        grid=(n_groups, m // tm),
        in_specs=[pl.BlockSpec((tm, gw), lambda j, i: (i, j)),
                  pl.BlockSpec((None, None, gw, gw), lambda j, i: (layer, j, 0, 0)),
                  pl.BlockSpec((tm, gw), lambda j, i: (i, j)),
                  pl.BlockSpec((1, gw), lambda j, i: (0, j))],
        out_specs=pl.BlockSpec((tm, gw), lambda j, i: (i, j)),
        out_shape=jax.ShapeDtypeStruct((m, d), res.dtype),
        scratch_shapes=[pltpu.VMEM((gw, gw), BF16)],
        compiler_params=_params(vmem, 2),
        name="pool_matmul",
    )(pooled, pool_w, res, scale.reshape(1, d))


def _gate_kernel(ba_ref, alog_ref, dtb_ref, o_ref, *, nv):
    n_chunks = ba_ref.shape[0] // CHUNK
    lane = lax.broadcasted_iota(jnp.int32, (CHUNK, 2 * nv), 1)
    pos = lax.broadcasted_iota(jnp.int32, (CHUNK, 2 * nv), 0)
    neg_a = -jnp.exp(alog_ref[...])
    dtb = dtb_ref[...]

    def body(c, carry):
        rows = pl.ds(pl.multiple_of(c * CHUNK, CHUNK), CHUNK)
        x = ba_ref[rows, :]
        beta = jax.nn.sigmoid(x)
        xa = x + dtb
        g = neg_a * (jnp.maximum(xa, 0.0) + jnp.log1p(jnp.exp(-jnp.abs(xa))))
        val = jnp.where(lane < nv, beta, g)
        val = jnp.where(jnp.logical_or(c > 0, pos >= PAD), val, 0.0)
        cs = val
        s = 1
        while s < CHUNK:
            cs = cs + jnp.where(pos >= s, pltpu.roll(cs, s, 0), 0.0)
            s *= 2
        o_ref[rows, :] = jnp.where(lane < nv, val, cs)
        return carry

    lax.fori_loop(0, n_chunks, body, 0)


def _gates(ba3, a_log, dt_bias):
    b, lp, w = ba3.shape
    nv = w // 2
    zeros = jnp.zeros((nv,), F32)
    alog = jnp.concatenate([zeros, a_log.astype(F32)]).reshape(1, w)
    dtb = jnp.concatenate([zeros, dt_bias.astype(F32)]).reshape(1, w)
    return pl.pallas_call(
        functools.partial(_gate_kernel, nv=nv),
        grid=(b,),
        in_specs=[pl.BlockSpec((None, lp, w), lambda bi: (bi, 0, 0)),
                  pl.BlockSpec((1, w), lambda bi: (0, 0)),
                  pl.BlockSpec((1, w), lambda bi: (0, 0))],
        out_specs=pl.BlockSpec((None, lp, w), lambda bi: (bi, 0, 0)),
        out_shape=jax.ShapeDtypeStruct((b, lp, w), F32),
        compiler_params=_params(8 * lp * w * 4, 1),
        name="delta_gates",
    )(ba3, alog, dtb)


def _gate_rows(gcb, nv):
    b, lp, _ = gcb.shape
    nkh = nv // HEADS_PER_KEY
    nc = lp // CHUNK
    gcr = gcb[:, :, nv:].reshape(b, nc, CHUNK, nkh, HEADS_PER_KEY).transpose(0, 3, 1, 4, 2)
    gl = gcr[..., CHUNK - 1:]
    return jnp.concatenate(
        [gcr.reshape(b, nkh, nc, PAIR_W),
         jnp.broadcast_to(gl, (b, nkh, nc, HEADS_PER_KEY, CHUNK)).reshape(b, nkh, nc, PAIR_W),
         jnp.broadcast_to(gl, (b, nkh, nc, HEADS_PER_KEY, DV)).reshape(b, nkh, nc, HEADS_PER_KEY * DV)], axis=-1)


def _block_diag(a, b):
    z = jnp.zeros_like(a)
    return jnp.concatenate([jnp.concatenate([a, z], axis=1), jnp.concatenate([z, b], axis=1)], axis=0)


def _delta_kernel(q_ref, k_ref, v_ref, z_ref, cwq_ref, cwk_ref, cwv_ref, gcb_ref, rows_ref, onorm_ref,
                  o_ref, u_s, wq_s, qk_s, kb_s, edg_s, st_s, *, nv, keys, group):
    kp = pl.program_id(1)
    n_chunks = q_ref.shape[0] // CHUNK
    vw = HEADS_PER_KEY * DV
    rowi = lax.broadcasted_iota(jnp.int32, (CHUNK, PAIR_W), 0)
    lanei = lax.broadcasted_iota(jnp.int32, (CHUNK, PAIR_W), 1)
    coli = lanei & (CHUNK - 1)
    left = lanei < CHUNK
    lower = rowi >= coli
    strict = rowi > coli
    eye = jnp.where(rowi == coli, 1.0, 0.0).astype(F32)
    lane_g = lax.broadcasted_iota(jnp.int32, (CHUNK, 2 * nv), 1)
    nt_dims = (((1,), (1,)), ((), ()))
    tn_dims = (((0,), (0,)), ((), ()))

    def mm_pair(x, y):
        ybd = jnp.concatenate([jnp.where(left, y, 0.0), jnp.where(left, 0.0, y)], axis=0).astype(BF16)
        return jnp.dot(x.astype(BF16), ybd, preferred_element_type=F32)

    def l2n(t, scale=1.0):
        return t * (lax.rsqrt(jnp.sum(t * t, axis=-1, keepdims=True) + EPS) * scale)

    def conv_silu(ref, cw_ref, rows, lb_rows, c0, width):
        x = jnp.concatenate([ref[lb_rows, c0:c0 + width], ref[rows, c0:c0 + width]], axis=0).astype(F32)
        w = cw_ref[:, c0:c0 + width]
        y = x[LOOKBACK:] * w[CONV_K - 1:CONV_K]
        for j in range(1, CONV_K):
            y = y + x[LOOKBACK - j:LOOKBACK - j + CHUNK] * w[CONV_K - 1 - j:CONV_K - j]
        return y * jax.nn.sigmoid(y)

    LOCAL_STAGES = 10

    def local_stages(g):
        chains = []
        for ci in range(group):
            c = g * group + ci
            r0 = pl.multiple_of(c * CHUNK, CHUNK)
            lb_rows = pl.ds(pl.multiple_of(jnp.maximum(r0 - LOOKBACK, 0), LOOKBACK), LOOKBACK)
            for e in range(keys):
                chains.append(dict(c=c, e=e, rows=pl.ds(r0, CHUNK), lb_rows=lb_rows))

        for ch in chains:
            e, rows, lb_rows = ch["e"], ch["rows"], ch["lb_rows"]
            q = l2n(conv_silu(q_ref, cwq_ref, rows, lb_rows, e * DK, DK), DK ** -0.5)
            k = l2n(conv_silu(k_ref, cwk_ref, rows, lb_rows, e * DK, DK))
            qb = q.astype(BF16)
            kb = k.astype(BF16)
            qkk = lax.dot_general(jnp.concatenate([qb, kb], axis=0), jnp.concatenate([kb, kb], axis=0),
                                  nt_dims, preferred_element_type=F32)
            ch.update(q=q, k=k, kb=kb, qkk=qkk)
        yield

        for ch in chains:
            gcb = gcb_ref[ch["rows"], :]

            def column(idx, gcb=gcb):
                return jnp.sum(jnp.where(lane_g == idx, gcb, 0.0), axis=1, keepdims=True)

            h0 = HEADS_PER_KEY * (keys * kp + ch["e"])
            beta_a, beta_b = column(h0), column(h0 + 1)
            gc_a, gc_b = column(nv + h0), column(nv + h0 + 1)
            gc = jnp.where(left, gc_a, gc_b)
            rr = rows_ref[ch["e"], pl.ds(ch["c"], 1), :]
            decay = jnp.where(lower, jnp.exp(jnp.where(lower, gc - rr[:, :PAIR_W], 0.0)), 0.0)
            n = -jnp.where(strict, (jnp.where(left, beta_a, beta_b) * ch["qkk"][CHUNK:]) * decay, 0.0)
            ch.update(beta_a=beta_a, beta_b=beta_b, eg_a=jnp.exp(gc_a), eg_b=jnp.exp(gc_b), decay=decay,
                      edg=jnp.exp(rr[:, PAIR_W:2 * PAIR_W] - gc), t=eye + n, p=n)

        assert CHUNK == 64

        def times_one_plus(x, p):
            return x + mm_pair(p, x)

        for ch in chains:
            ch["p2"] = mm_pair(ch["p"], ch["p"])
        yield
        for ch in chains:
            ch["p4"] = mm_pair(ch["p2"], ch["p2"])
        for ch in chains:
            ch["a1"] = times_one_plus(ch["t"], ch["p2"])
        yield
        for ch in chains:
            ch["p8"] = mm_pair(ch["p4"], ch["p4"])
        yield
        for ch in chains:
            ch["p16"] = mm_pair(ch["p8"], ch["p8"])
        for ch in chains:
            ch["a2"] = times_one_plus(eye + ch["p4"], ch["p8"])
        yield
        for ch in chains:
            ch["p32"] = mm_pair(ch["p16"], ch["p16"])
        for ch in chains:
            ch["a12"] = mm_pair(ch["a1"], ch["a2"])
        yield
        for ch in chains:
            ch["a3"] = times_one_plus(eye + ch["p16"], ch["p32"])
        yield
        for ch in chains:
            ch["t"] = mm_pair(ch["a12"], ch["a3"])
        yield

        for ch in chains:
            e, rows = ch["e"], ch["rows"]
            v = conv_silu(v_ref, cwv_ref, rows, ch["lb_rows"], e * vw, vw)
            k = ch["k"]
            rhs_a = jnp.concatenate([v[:, :DV] * ch["beta_a"], k * (ch["beta_a"] * ch["eg_a"])], axis=1)
            rhs_b = jnp.concatenate([v[:, DV:] * ch["beta_b"], k * (ch["beta_b"] * ch["eg_b"])], axis=1)
            ch["uw"] = jnp.dot(ch["t"].astype(BF16), _block_diag(rhs_a.astype(BF16), rhs_b.astype(BF16)),
                               preferred_element_type=F32)
        yield

        for ch in chains:
            e, rows, uw, q = ch["e"], ch["rows"], ch["uw"], ch["q"]
            wq0 = pl.multiple_of(ch["c"] * (2 * CHUNK), 2 * CHUNK)
            u_s[e, rows, :] = jnp.concatenate([uw[:, :DV], uw[:, 2 * DV:3 * DV]], axis=1)
            wq_s[e, pl.ds(wq0, CHUNK), :] = jnp.concatenate([uw[:, DV:2 * DV], uw[:, 3 * DV:]], axis=1).astype(BF16)
            wq_s[e, pl.ds(wq0 + CHUNK, CHUNK), :] = jnp.concatenate([q * ch["eg_a"], q * ch["eg_b"]],
                                                                     axis=1).astype(BF16)
            qk_s[e, rows, :] = (ch["qkk"][:CHUNK] * ch["decay"]).astype(BF16)
            kb_s[e, rows, :] = ch["kb"]
            edg_s[e, rows, :] = ch["edg"]

    onorm = onorm_ref[...]

    def scan_stages(g):
        states = [st_s[e] for e in range(keys)]
        for ci in range(group):
            c = g * group + ci
            rows = pl.ds(pl.multiple_of(c * CHUNK, CHUNK), CHUNK)
            wq_rows = pl.ds(pl.multiple_of(c * (2 * CHUNK), 2 * CHUNK), 2 * CHUNK)
            r1s, v_news, outs = [], [], []
            for e in range(keys):
                sb = states[e].astype(BF16)
                r1s.append(jnp.dot(wq_s[e, wq_rows, :], _block_diag(sb[:, :DV], sb[:, DV:]),
                                   preferred_element_type=F32))
            yield
            for e in range(keys):
                v_news.append(u_s[e, rows, :] - r1s[e][:CHUNK])
            for e in range(keys):
                v_new = v_news[e]
                edg = edg_s[e, rows, :]
                edg_sw = pltpu.roll(edg, CHUNK, 1)
                dv = jnp.concatenate([v_new[:, :DV] * jnp.where(left, edg, edg_sw),
                                      v_new[:, DV:] * jnp.where(left, edg_sw, edg)], axis=1).astype(BF16)
                egl = jnp.exp(rows_ref[e, pl.ds(c, 1), 2 * PAIR_W:])
                states[e] = states[e] * egl + lax.dot_general(kb_s[e, rows, :], dv, tn_dims,
                                                              preferred_element_type=F32)
            for e in range(keys):
                vb = v_news[e].astype(BF16)
                outs.append(r1s[e][CHUNK:] + jnp.dot(qk_s[e, rows, :], _block_diag(vb[:, :DV], vb[:, DV:]),
                                                     preferred_element_type=F32))
            yield
            for e in range(keys):
                for j in range(HEADS_PER_KEY):
                    oj = outs[e][:, j * DV:(j + 1) * DV]
                    on = oj * lax.rsqrt(jnp.mean(oj * oj, axis=-1, keepdims=True) + EPS) * onorm
                    c0 = e * vw + j * DV
                    z = z_ref[rows, c0:c0 + DV].astype(F32)
                    o_ref[rows, c0:c0 + DV] = (on * (z * jax.nn.sigmoid(z))).astype(o_ref.dtype)
        for e in range(keys):
            st_s[e] = states[e]

    def run(local_gen, scan_gen):
        gens = [g for g in (local_gen, scan_gen) if g is not None]
        counts = {id(local_gen): LOCAL_STAGES, id(scan_gen): 2 * group + 1}
        order = sorted((float(s + 0.5) / counts[id(g)], n, g) for n, g in enumerate(gens)
                       for s in range(counts[id(g)]))
        for _, _, g in order:
            next(g, None)
        for g in gens:
            assert next(g, "done") == "done"

    n_groups = n_chunks // group
    st_s[...] = jnp.zeros_like(st_s)
    run(local_stages(0), None)

    def merged(g, carry):
        run(local_stages(g + 1), scan_stages(g))
        return carry

    lax.fori_loop(0, n_groups - 1, merged, 0)
    run(None, scan_stages(n_groups - 1))


def _delta_rule(proj3, conv_w, layer, gcb, rows, out_norm, *, nk_heads, nv, keys=2, group=3):
    b, lp, _ = proj3.shape
    assert nv == HEADS_PER_KEY * nk_heads and nk_heads % keys == 0 and PAIR_W == 128
    assert (lp // CHUNK) % group == 0
    kw = keys * DK
    vw = keys * HEADS_PER_KEY * DV
    key_dim = nk_heads * DK
    k_blk = key_dim // kw
    v_blk = 2 * key_dim // vw
    z_blk = (2 * key_dim + nv * DV) // vw
    n_chunks = lp // CHUNK
    pvw = HEADS_PER_KEY * DV
    scratch = [pltpu.VMEM((keys, lp, pvw), F32),
               pltpu.VMEM((keys, 2 * lp, pvw), BF16),
               pltpu.VMEM((keys, lp, PAIR_W), BF16),
               pltpu.VMEM((keys, lp, DK), BF16),
               pltpu.VMEM((keys, lp, PAIR_W), F32),
               pltpu.VMEM((keys, DK, pvw), F32)]
    vmem = (2 * lp * (2 * kw + 3 * vw) * 2 + 2 * lp * 2 * nv * 4
            + keys * lp * (pvw * 4 + 2 * pvw * 2 + PAIR_W * 2 + DK * 2 + PAIR_W * 4) + 8 * MIB)
    return pl.pallas_call(
        functools.partial(_delta_kernel, nv=nv, keys=keys, group=group),
        grid=(b, nk_heads // keys),
        in_specs=[pl.BlockSpec((None, lp, kw), lambda bi, kp: (bi, 0, kp)),
                  pl.BlockSpec((None, lp, kw), lambda bi, kp: (bi, 0, k_blk + kp)),
                  pl.BlockSpec((None, lp, vw), lambda bi, kp: (bi, 0, v_blk + kp)),
                  pl.BlockSpec((None, lp, vw), lambda bi, kp: (bi, 0, z_blk + kp)),
                  pl.BlockSpec((None, CONV_K, kw), lambda bi, kp: (layer, 0, kp)),
                  pl.BlockSpec((None, CONV_K, kw), lambda bi, kp: (layer, 0, k_blk + kp)),
                  pl.BlockSpec((None, CONV_K, vw), lambda bi, kp: (layer, 0, v_blk + kp)),
                  pl.BlockSpec((None, lp, 2 * nv), lambda bi, kp: (bi, 0, 0)),
                  pl.BlockSpec((None, keys, n_chunks, rows.shape[-1]), lambda bi, kp: (bi, kp, 0, 0)),
                  pl.BlockSpec((1, DV), lambda bi, kp: (0, 0))],
        out_specs=pl.BlockSpec((None, lp, vw), lambda bi, kp: (bi, 0, kp)),
        out_shape=jax.ShapeDtypeStruct((b, lp, nv * DV), BF16),
        scratch_shapes=scratch,
        compiler_params=_params(vmem, 2),
        name="delta_rule",
    )(proj3, proj3, proj3, proj3, conv_w, conv_w, conv_w, gcb, rows, out_norm.reshape(1, DV))


def _pool_kernel(lb_ref, x_ref, g_ref, o_ref, *, col_blk):
    i = pl.program_id(1)
    tr, d = x_ref.shape
    gw = d // len(POOL_WINDOWS)
    x = x_ref[...]
    lb = lb_ref[...]
    inv = lax.rsqrt(jnp.mean(x * x, axis=-1, keepdims=True) + EPS)
    inv_lb = lax.rsqrt(jnp.mean(lb * lb, axis=-1, keepdims=True) + EPS)
    row = i * tr - LOOKBACK + lax.broadcasted_iota(jnp.int32, (tr + LOOKBACK, 1), 0)
    valid = row >= PAD
    pos = (row[LOOKBACK:] - (PAD - 1)).astype(F32)
    for gi, win in enumerate(POOL_WINDOWS):
        div = jnp.where(valid[LOOKBACK:], jnp.minimum(pos, float(win)), 1.0)
        for cb in range(gw // col_blk):
            c0 = gi * gw + cb * col_blk
            gv = g_ref[:, c0:c0 + col_blk]
            xe = jnp.concatenate([lb[:, c0:c0 + col_blk] * inv_lb * gv, x[:, c0:c0 + col_blk] * inv * gv], axis=0)
            xe = jnp.where(valid, xe, 0.0)
            s = xe
            sh = 1
            while sh < win:
                s = s + pltpu.roll(s, sh, 0)
                sh *= 2
            o_ref[:, c0:c0 + col_blk] = (s[LOOKBACK:] / div - xe[LOOKBACK:]).astype(o_ref.dtype)


def _pool(h3, g):
    b, lp, d = h3.shape
    tr = CHUNK
    per = tr // LOOKBACK
    return pl.pallas_call(
        functools.partial(_pool_kernel, col_blk=512),
        grid=(b, lp // tr),
        in_specs=[pl.BlockSpec((None, LOOKBACK, d), lambda bi, i: (bi, jnp.maximum(i * per - 1, 0), 0)),
                  pl.BlockSpec((None, tr, d), lambda bi, i: (bi, i, 0)),
                  pl.BlockSpec((1, d), lambda bi, i: (0, 0))],
        out_specs=pl.BlockSpec((None, tr, d), lambda bi, i: (bi, i, 0)),
        out_shape=jax.ShapeDtypeStruct((b, lp, d), BF16),
        compiler_params=_params(8 * tr * d * 4, 2),
        name="pool",
    )(h3, h3, g.reshape(1, d))


def kernel(x, meta_tokens, mix_norm, dn_w_in, dn_conv_w, dn_a_log, dn_dt_bias, dn_out_norm, dn_w_out, pool_w,
           pool_scale, mlp_norm, w_up, w_down, final_norm):
    b, seq, d = x.shape
    nv = dn_a_log.shape[1]
    val_dim = nv * DV
    key_dim = (dn_conv_w.shape[2] - val_dim) // 2
    nk_heads = key_dim // DK
    qkvz = 2 * key_dim + 2 * val_dim
    d_ff = w_up.shape[2]
    depth = mix_norm.shape[0]
    lp = PAD + N_META + seq
    m = b * lp
    tm = m // 8
    tm_stream = m // 4
    assert lp % CHUNK == 0 and tm % 16 == 0 and tm_stream % 16 == 0 and dn_w_in.shape[2] == qkvz + 2 * nv

    meta = jnp.broadcast_to(meta_tokens.astype(x.dtype)[None], (b, N_META, d))
    h = jnp.concatenate([jnp.zeros((b, PAD, d), x.dtype), meta, x], axis=1).reshape(m, d)

    for i in range(depth):
        j = i // 2
        if i % 2 == 0:
            hn = _rmsnorm(h, mix_norm[i], BF16)
            proj = _matmul_stream(hn, dn_w_in, j, n=qkvz, tn=512, tm=tm_stream, out_dtype=BF16)
            ba = _matmul_stream(hn, dn_w_in, j, n=2 * nv, n_off=qkvz, tn=2 * nv, tm=tm_stream, out_dtype=F32)
            gcb = _gates(ba.reshape(b, lp, 2 * nv), dn_a_log[j], dn_dt_bias[j])
            og = _delta_rule(proj.reshape(b, lp, qkvz), dn_conv_w, j, gcb, _gate_rows(gcb, nv), dn_out_norm[j],
                             nk_heads=nk_heads, nv=nv)
            h = _matmul_chunked(og.reshape(m, val_dim), dn_w_out, j, h, n=d, tn=512, tm=tm_stream, kc=d)
        else:
            pooled = _pool(h.reshape(b, lp, d), mix_norm[i])
            h = _pool_matmul(pooled.reshape(m, d), pool_w, j, pool_scale[j], h, tm=tm)
        hn = _rmsnorm(h, mlp_norm[i], BF16)
        act = _matmul_stream(hn, w_up, i, n=d_ff, tn=512, tm=tm_stream, out_dtype=BF16, relu2=True)
        h = _matmul_chunked(act, w_down, i, h, n=d, tn=512, tm=tm_stream, kc=d)

    return _final_norm(h.reshape(b, lp, d), final_norm, seq)
```

```python
import functools

import jax
import jax.numpy as jnp
from jax import lax
from jax.experimental import pallas as pl
from jax.experimental.pallas import tpu as pltpu

F32 = jnp.float32
BF16 = jnp.bfloat16

N_META = 16
CHUNK = 64
DK = 128
DV = 128
CONV_K = 4
POOL_WINDOWS = (2, 4, 8, 16)
EPS = 1e-6
PAD = (-N_META) % CHUNK
LOOKBACK = 16
HEADS_PER_KEY = 2
PAIR_W = HEADS_PER_KEY * CHUNK

V7X_VMEM_BYTES = 64 * 1024 * 1024
MIB = 1024 * 1024


def _params(vmem_bytes, n_axes):
    limit = min(int(vmem_bytes) + 4 * MIB, V7X_VMEM_BYTES - 6 * MIB)
    return pltpu.CompilerParams(dimension_semantics=("arbitrary",) * n_axes, vmem_limit_bytes=limit)


def _rmsnorm_kernel(x_ref, g_ref, o_ref):
    x = x_ref[...]
    ms = jnp.mean(x * x, axis=-1, keepdims=True)
    o_ref[...] = (x * lax.rsqrt(ms + EPS) * g_ref[...]).astype(o_ref.dtype)


def _rmsnorm(x, g, out_dtype, tr=256):
    m, d = x.shape
    return pl.pallas_call(
        _rmsnorm_kernel,
        grid=(m // tr,),
        in_specs=[pl.BlockSpec((tr, d), lambda i: (i, 0)), pl.BlockSpec((1, d), lambda i: (0, 0))],
        out_specs=pl.BlockSpec((tr, d), lambda i: (i, 0)),
        out_shape=jax.ShapeDtypeStruct((m, d), out_dtype),
        compiler_params=_params(6 * tr * d * 4, 1),
        name="rmsnorm",
    )(x, g.reshape(1, d))


def _final_norm_kernel(main_ref, next_ref, g_ref, o_ref):
    nb = o_ref.shape[0]
    g = g_ref[...]
    for c in range(nb):
        x = main_ref[c + 1] if c + 1 < nb else next_ref[0]
        ms = jnp.mean(x * x, axis=-1, keepdims=True)
        o_ref[c] = (x * lax.rsqrt(ms + EPS) * g).astype(o_ref.dtype)


def _final_norm(h3, g, seq, nb=8):
    b, lp, d = h3.shape
    assert lp - seq == CHUNK and seq % (nb * CHUNK) == 0
    h4 = h3.reshape(b, lp // CHUNK, CHUNK, d)
    out = pl.pallas_call(
        _final_norm_kernel,
        grid=(b, seq // (nb * CHUNK)),
        in_specs=[pl.BlockSpec((None, nb, CHUNK, d), lambda bi, i: (bi, i, 0, 0)),
                  pl.BlockSpec((None, 1, CHUNK, d), lambda bi, i: (bi, (i + 1) * nb, 0, 0)),
                  pl.BlockSpec((1, d), lambda bi, i: (0, 0))],
        out_specs=pl.BlockSpec((None, nb, CHUNK, d), lambda bi, i: (bi, i, 0, 0)),
        out_shape=jax.ShapeDtypeStruct((b, seq // CHUNK, CHUNK, d), h3.dtype),
        compiler_params=_params((4 * nb + 4) * CHUNK * d * 4, 2),
        name="final_norm",
    )(h4, h4, g.reshape(1, d))
    return out.reshape(b, seq, d)


def _pool_mm_kernel(a_ref, w_ref, res_ref, scale_ref, o_ref, wb_ref):
    @pl.when(pl.program_id(1) == 0)
    def _cast_weights():
        wb_ref[...] = w_ref[...].astype(BF16)

    y = jnp.dot(a_ref[...], wb_ref[...], preferred_element_type=F32)
    o_ref[...] = (res_ref[...] + y * scale_ref[...]).astype(o_ref.dtype)


def _mm_stream_kernel(*refs, relu2, has_prev):
    if has_prev:
        a_ref, w_ref, prev_ref, o_ref = refs
    else:
        a_ref, w_ref, o_ref = refs
    y = jnp.dot(a_ref[...], w_ref[...].astype(BF16), preferred_element_type=F32)
    if relu2:
        r = jnp.maximum(y, 0.0)
        y = r * r
    if has_prev:
        y = prev_ref[...] + y
    o_ref[...] = y.astype(o_ref.dtype)


def _matmul_stream(a, w, layer, *, n, tn, tm, out_dtype, n_off=0, relu2=False, prev=None, kc=None, k_blk=0,
                   a_buffers=1):
    m, kdim = a.shape
    kc = kdim if kc is None else kc
    assert m % tm == 0 and n % tn == 0 and n_off % tn == 0 and kdim % kc == 0
    joff = n_off // tn
    osz = jnp.dtype(out_dtype).itemsize
    vmem = (2 * kc * tn * 4 + kc * tn * 2 + a_buffers * tm * kc * 2 + 2 * tm * tn * osz + tm * tn * 4
            + (2 * tm * tn * 4 if prev is not None else 0))
    in_specs = [pl.BlockSpec((tm, kc), lambda i, j: (i, k_blk), pipeline_mode=pl.Buffered(a_buffers)),
                pl.BlockSpec((None, kc, tn), lambda i, j: (layer, k_blk, j + joff))]
    args = [a, w]
    if prev is not None:
        in_specs.append(pl.BlockSpec((tm, tn), lambda i, j: (i, j)))
        args.append(prev)
    return pl.pallas_call(
        functools.partial(_mm_stream_kernel, relu2=relu2, has_prev=prev is not None),
        grid=(m // tm, n // tn),
        in_specs=in_specs,
        out_specs=pl.BlockSpec((tm, tn), lambda i, j: (i, j)),
        out_shape=jax.ShapeDtypeStruct((m, n), out_dtype),
        compiler_params=_params(vmem, 2),
        name="matmul_stream",
    )(*args)


def _matmul_chunked(a, w, layer, res, *, n, tn, tm, kc):
    for k_blk in range(a.shape[1] // kc):
        res = _matmul_stream(a, w, layer, n=n, tn=tn, tm=tm, out_dtype=res.dtype, prev=res, kc=kc, k_blk=k_blk,
                             a_buffers=2)
    return res


def _pool_matmul(pooled, pool_w, layer, scale, res, *, tm):
    m, d = pooled.shape
    n_groups, gw = pool_w.shape[1], pool_w.shape[2]
    vmem = 2 * gw * gw * 4 + gw * gw * 2 + 2 * tm * gw * 2 + 6 * tm * gw * 4
    return pl.pallas_call(
        _pool_mm_kernel,
        grid=(n_groups, m // tm),
        in_specs=[pl.BlockSpec((tm, gw), lambda j, i: (i, j)),
                  pl.BlockSpec((None, None, gw, gw), lambda j, i: (layer, j, 0, 0)),
                  pl.BlockSpec((tm, gw), lambda j, i: (i, j)),
                  pl.BlockSpec((1, gw), lambda j, i: (0, j))],
        out_specs=pl.BlockSpec((tm, gw), lambda j, i: (i, j)),
        out_shape=jax.ShapeDtypeStruct((m, d), res.dtype),
        scratch_shapes=[pltpu.VMEM((gw, gw), BF16)],
        compiler_params=_params(vmem, 2),
        name="pool_matmul",
    )(pooled, pool_w, res, scale.reshape(1, d))


def _gate_kernel(ba_ref, alog_ref, dtb_ref, o_ref, *, nv):
    n_chunks = ba_ref.shape[0] // CHUNK
    lane = lax.broadcasted_iota(jnp.int32, (CHUNK, 2 * nv), 1)
    pos = lax.broadcasted_iota(jnp.int32, (CHUNK, 2 * nv), 0)
    neg_a = -jnp.exp(alog_ref[...])
    dtb = dtb_ref[...]

    def body(c, carry):
        rows = pl.ds(pl.multiple_of(c * CHUNK, CHUNK), CHUNK)
        x = ba_ref[rows, :]
        beta = jax.nn.sigmoid(x)
        xa = x + dtb
        g = neg_a * (jnp.maximum(xa, 0.0) + jnp.log1p(jnp.exp(-jnp.abs(xa))))
        val = jnp.where(lane < nv, beta, g)
        val = jnp.where(jnp.logical_or(c > 0, pos >= PAD), val, 0.0)
        cs = val
        s = 1
        while s < CHUNK:
            cs = cs + jnp.where(pos >= s, pltpu.roll(cs, s, 0), 0.0)
            s *= 2
        o_ref[rows, :] = jnp.where(lane < nv, val, cs)
        return carry

    lax.fori_loop(0, n_chunks, body, 0)


def _gates(ba3, a_log, dt_bias):
    b, lp, w = ba3.shape
    nv = w // 2
    zeros = jnp.zeros((nv,), F32)
    alog = jnp.concatenate([zeros, a_log.astype(F32)]).reshape(1, w)
    dtb = jnp.concatenate([zeros, dt_bias.astype(F32)]).reshape(1, w)
    return pl.pallas_call(
        functools.partial(_gate_kernel, nv=nv),
        grid=(b,),
        in_specs=[pl.BlockSpec((None, lp, w), lambda bi: (bi, 0, 0)),
                  pl.BlockSpec((1, w), lambda bi: (0, 0)),
                  pl.BlockSpec((1, w), lambda bi: (0, 0))],
        out_specs=pl.BlockSpec((None, lp, w), lambda bi: (bi, 0, 0)),
        out_shape=jax.ShapeDtypeStruct((b, lp, w), F32),
        compiler_params=_params(8 * lp * w * 4, 1),
        name="delta_gates",
    )(ba3, alog, dtb)


def _gate_rows(gcb, nv):
    b, lp, _ = gcb.shape
    nkh = nv // HEADS_PER_KEY
    nc = lp // CHUNK
    gcr = gcb[:, :, nv:].reshape(b, nc, CHUNK, nkh, HEADS_PER_KEY).transpose(0, 3, 1, 4, 2)
    gl = gcr[..., CHUNK - 1:]
    return jnp.concatenate(
        [gcr.reshape(b, nkh, nc, PAIR_W),
         jnp.broadcast_to(gl, (b, nkh, nc, HEADS_PER_KEY, CHUNK)).reshape(b, nkh, nc, PAIR_W),
         jnp.broadcast_to(gl, (b, nkh, nc, HEADS_PER_KEY, DV)).reshape(b, nkh, nc, HEADS_PER_KEY * DV)], axis=-1)


def _block_diag(a, b):
    z = jnp.zeros_like(a)
    return jnp.concatenate([jnp.concatenate([a, z], axis=1), jnp.concatenate([z, b], axis=1)], axis=0)


def _delta_kernel(q_ref, k_ref, v_ref, z_ref, cwq_ref, cwk_ref, cwv_ref, gcb_ref, rows_ref, onorm_ref,
                  o_ref, u_s, wq_s, qk_s, kb_s, edg_s, st_s, *, nv, keys, group):
    kp = pl.program_id(1)
    n_chunks = q_ref.shape[0] // CHUNK
    vw = HEADS_PER_KEY * DV
    rowi = lax.broadcasted_iota(jnp.int32, (CHUNK, PAIR_W), 0)
    lanei = lax.broadcasted_iota(jnp.int32, (CHUNK, PAIR_W), 1)
    coli = lanei & (CHUNK - 1)
    left = lanei < CHUNK
    lower = rowi >= coli
    strict = rowi > coli
    eye = jnp.where(rowi == coli, 1.0, 0.0).astype(F32)
    lane_g = lax.broadcasted_iota(jnp.int32, (CHUNK, 2 * nv), 1)
    nt_dims = (((1,), (1,)), ((), ()))
    tn_dims = (((0,), (0,)), ((), ()))

    def mm_pair(x, y):
        ybd = jnp.concatenate([jnp.where(left, y, 0.0), jnp.where(left, 0.0, y)], axis=0).astype(BF16)
        return jnp.dot(x.astype(BF16), ybd, preferred_element_type=F32)

    def l2n(t, scale=1.0):
        return t * (lax.rsqrt(jnp.sum(t * t, axis=-1, keepdims=True) + EPS) * scale)

    def conv_silu(ref, cw_ref, rows, lb_rows, c0, width):
        x = jnp.concatenate([ref[lb_rows, c0:c0 + width], ref[rows, c0:c0 + width]], axis=0).astype(F32)
        w = cw_ref[:, c0:c0 + width]
        y = x[LOOKBACK:] * w[CONV_K - 1:CONV_K]
        for j in range(1, CONV_K):
            y = y + x[LOOKBACK - j:LOOKBACK - j + CHUNK] * w[CONV_K - 1 - j:CONV_K - j]
        return y * jax.nn.sigmoid(y)

    LOCAL_STAGES = 10

    def local_stages(g):
        chains = []
        for ci in range(group):
            c = g * group + ci
            r0 = pl.multiple_of(c * CHUNK, CHUNK)
            lb_rows = pl.ds(pl.multiple_of(jnp.maximum(r0 - LOOKBACK, 0), LOOKBACK), LOOKBACK)
            for e in range(keys):
                chains.append(dict(c=c, e=e, rows=pl.ds(r0, CHUNK), lb_rows=lb_rows))

        for ch in chains:
            e, rows, lb_rows = ch["e"], ch["rows"], ch["lb_rows"]
            q = l2n(conv_silu(q_ref, cwq_ref, rows, lb_rows, e * DK, DK), DK ** -0.5)
            k = l2n(conv_silu(k_ref, cwk_ref, rows, lb_rows, e * DK, DK))
            qb = q.astype(BF16)
            kb = k.astype(BF16)
            qkk = lax.dot_general(jnp.concatenate([qb, kb], axis=0), jnp.concatenate([kb, kb], axis=0),
                                  nt_dims, preferred_element_type=F32)
            ch.update(q=q, k=k, kb=kb, qkk=qkk)
        yield

        for ch in chains:
            gcb = gcb_ref[ch["rows"], :]

            def column(idx, gcb=gcb):
                return jnp.sum(jnp.where(lane_g == idx, gcb, 0.0), axis=1, keepdims=True)

            h0 = HEADS_PER_KEY * (keys * kp + ch["e"])
            beta_a, beta_b = column(h0), column(h0 + 1)
            gc_a, gc_b = column(nv + h0), column(nv + h0 + 1)
            gc = jnp.where(left, gc_a, gc_b)
            rr = rows_ref[ch["e"], pl.ds(ch["c"], 1), :]
            decay = jnp.where(lower, jnp.exp(jnp.where(lower, gc - rr[:, :PAIR_W], 0.0)), 0.0)
            n = -jnp.where(strict, (jnp.where(left, beta_a, beta_b) * ch["qkk"][CHUNK:]) * decay, 0.0)
            ch.update(beta_a=beta_a, beta_b=beta_b, eg_a=jnp.exp(gc_a), eg_b=jnp.exp(gc_b), decay=decay,
                      edg=jnp.exp(rr[:, PAIR_W:2 * PAIR_W] - gc), t=eye + n, p=n)

        assert CHUNK == 64

        def times_one_plus(x, p):
            return x + mm_pair(p, x)

        for ch in chains:
            ch["p2"] = mm_pair(ch["p"], ch["p"])
        yield
        for ch in chains:
            ch["p4"] = mm_pair(ch["p2"], ch["p2"])
        for ch in chains:
            ch["a1"] = times_one_plus(ch["t"], ch["p2"])
        yield
        for ch in chains:
            ch["p8"] = mm_pair(ch["p4"], ch["p4"])
        yield
        for ch in chains:
            ch["p16"] = mm_pair(ch["p8"], ch["p8"])
        for ch in chains:
            ch["a2"] = times_one_plus(eye + ch["p4"], ch["p8"])
        yield
        for ch in chains:
            ch["p32"] = mm_pair(ch["p16"], ch["p16"])
        for ch in chains:
            ch["a12"] = mm_pair(ch["a1"], ch["a2"])
        yield
        for ch in chains:
            ch["a3"] = times_one_plus(eye + ch["p16"], ch["p32"])
        yield
        for ch in chains:
            ch["t"] = mm_pair(ch["a12"], ch["a3"])
        yield

        for ch in chains:
            e, rows = ch["e"], ch["rows"]
            v = conv_silu(v_ref, cwv_ref, rows, ch["lb_rows"], e * vw, vw)
            k = ch["k"]
            rhs_a = jnp.concatenate([v[:, :DV] * ch["beta_a"], k * (ch["beta_a"] * ch["eg_a"])], axis=1)
            rhs_b = jnp.concatenate([v[:, DV:] * ch["beta_b"], k * (ch["beta_b"] * ch["eg_b"])], axis=1)
            ch["uw"] = jnp.dot(ch["t"].astype(BF16), _block_diag(rhs_a.astype(BF16), rhs_b.astype(BF16)),
                               preferred_element_type=F32)
        yield

        for ch in chains:
            e, rows, uw, q = ch["e"], ch["rows"], ch["uw"], ch["q"]
            wq0 = pl.multiple_of(ch["c"] * (2 * CHUNK), 2 * CHUNK)
            u_s[e, rows, :] = jnp.concatenate([uw[:, :DV], uw[:, 2 * DV:3 * DV]], axis=1)
            wq_s[e, pl.ds(wq0, CHUNK), :] = jnp.concatenate([uw[:, DV:2 * DV], uw[:, 3 * DV:]], axis=1).astype(BF16)
            wq_s[e, pl.ds(wq0 + CHUNK, CHUNK), :] = jnp.concatenate([q * ch["eg_a"], q * ch["eg_b"]],
                                                                     axis=1).astype(BF16)
            qk_s[e, rows, :] = (ch["qkk"][:CHUNK] * ch["decay"]).astype(BF16)
            kb_s[e, rows, :] = ch["kb"]
            edg_s[e, rows, :] = ch["edg"]

    onorm = onorm_ref[...]

    def scan_stages(g):
        states = [st_s[e] for e in range(keys)]
        for ci in range(group):
            c = g * group + ci
            rows = pl.ds(pl.multiple_of(c * CHUNK, CHUNK), CHUNK)
            wq_rows = pl.ds(pl.multiple_of(c * (2 * CHUNK), 2 * CHUNK), 2 * CHUNK)
            r1s, v_news, outs = [], [], []
            for e in range(keys):
                sb = states[e].astype(BF16)
                r1s.append(jnp.dot(wq_s[e, wq_rows, :], _block_diag(sb[:, :DV], sb[:, DV:]),
                                   preferred_element_type=F32))
            yield
            for e in range(keys):
                v_news.append(u_s[e, rows, :] - r1s[e][:CHUNK])
            for e in range(keys):
                v_new = v_news[e]
                edg = edg_s[e, rows, :]
                edg_sw = pltpu.roll(edg, CHUNK, 1)
                dv = jnp.concatenate([v_new[:, :DV] * jnp.where(left, edg, edg_sw),
                                      v_new[:, DV:] * jnp.where(left, edg_sw, edg)], axis=1).astype(BF16)
                egl = jnp.exp(rows_ref[e, pl.ds(c, 1), 2 * PAIR_W:])
                states[e] = states[e] * egl + lax.dot_general(kb_s[e, rows, :], dv, tn_dims,
                                                              preferred_element_type=F32)
            for e in range(keys):
                vb = v_news[e].astype(BF16)
                outs.append(r1s[e][CHUNK:] + jnp.dot(qk_s[e, rows, :], _block_diag(vb[:, :DV], vb[:, DV:]),
                                                     preferred_element_type=F32))
            yield
            for e in range(keys):
                for j in range(HEADS_PER_KEY):
                    oj = outs[e][:, j * DV:(j + 1) * DV]
                    on = oj * lax.rsqrt(jnp.mean(oj * oj, axis=-1, keepdims=True) + EPS) * onorm
                    c0 = e * vw + j * DV
                    z = z_ref[rows, c0:c0 + DV].astype(F32)
                    o_ref[rows, c0:c0 + DV] = (on * (z * jax.nn.sigmoid(z))).astype(o_ref.dtype)
        for e in range(keys):
            st_s[e] = states[e]

    def run(local_gen, scan_gen):
        gens = [g for g in (local_gen, scan_gen) if g is not None]
        counts = {id(local_gen): LOCAL_STAGES, id(scan_gen): 2 * group + 1}
        order = sorted((float(s + 0.5) / counts[id(g)], n, g) for n, g in enumerate(gens)
                       for s in range(counts[id(g)]))
        for _, _, g in order:
            next(g, None)
        for g in gens:
            assert next(g, "done") == "done"

    n_groups = n_chunks // group
    st_s[...] = jnp.zeros_like(st_s)
    run(local_stages(0), None)

    def merged(g, carry):
        run(local_stages(g + 1), scan_stages(g))
        return carry

    lax.fori_loop(0, n_groups - 1, merged, 0)
    run(None, scan_stages(n_groups - 1))


def _delta_rule(proj3, conv_w, layer, gcb, rows, out_norm, *, nk_heads, nv, keys=2, group=3):
    b, lp, _ = proj3.shape
    assert nv == HEADS_PER_KEY * nk_heads and nk_heads % keys == 0 and PAIR_W == 128
    assert (lp // CHUNK) % group == 0
    kw = keys * DK
    vw = keys * HEADS_PER_KEY * DV
    key_dim = nk_heads * DK
    k_blk = key_dim // kw
    v_blk = 2 * key_dim // vw
    z_blk = (2 * key_dim + nv * DV) // vw
    n_chunks = lp // CHUNK
    pvw = HEADS_PER_KEY * DV
    scratch = [pltpu.VMEM((keys, lp, pvw), F32),
               pltpu.VMEM((keys, 2 * lp, pvw), BF16),
               pltpu.VMEM((keys, lp, PAIR_W), BF16),
               pltpu.VMEM((keys, lp, DK), BF16),
               pltpu.VMEM((keys, lp, PAIR_W), F32),
               pltpu.VMEM((keys, DK, pvw), F32)]
    vmem = (2 * lp * (2 * kw + 3 * vw) * 2 + 2 * lp * 2 * nv * 4
            + keys * lp * (pvw * 4 + 2 * pvw * 2 + PAIR_W * 2 + DK * 2 + PAIR_W * 4) + 8 * MIB)
    return pl.pallas_call(
        functools.partial(_delta_kernel, nv=nv, keys=keys, group=group),
        grid=(b, nk_heads // keys),
        in_specs=[pl.BlockSpec((None, lp, kw), lambda bi, kp: (bi, 0, kp)),
                  pl.BlockSpec((None, lp, kw), lambda bi, kp: (bi, 0, k_blk + kp)),
                  pl.BlockSpec((None, lp, vw), lambda bi, kp: (bi, 0, v_blk + kp)),
                  pl.BlockSpec((None, lp, vw), lambda bi, kp: (bi, 0, z_blk + kp)),
                  pl.BlockSpec((None, CONV_K, kw), lambda bi, kp: (layer, 0, kp)),
                  pl.BlockSpec((None, CONV_K, kw), lambda bi, kp: (layer, 0, k_blk + kp)),
                  pl.BlockSpec((None, CONV_K, vw), lambda bi, kp: (layer, 0, v_blk + kp)),
                  pl.BlockSpec((None, lp, 2 * nv), lambda bi, kp: (bi, 0, 0)),
                  pl.BlockSpec((None, keys, n_chunks, rows.shape[-1]), lambda bi, kp: (bi, kp, 0, 0)),
                  pl.BlockSpec((1, DV), lambda bi, kp: (0, 0))],
        out_specs=pl.BlockSpec((None, lp, vw), lambda bi, kp: (bi, 0, kp)),
        out_shape=jax.ShapeDtypeStruct((b, lp, nv * DV), BF16),
        scratch_shapes=scratch,
        compiler_params=_params(vmem, 2),
        name="delta_rule",
    )(proj3, proj3, proj3, proj3, conv_w, conv_w, conv_w, gcb, rows, out_norm.reshape(1, DV))


def _pool_kernel(lb_ref, x_ref, g_ref, o_ref, *, col_blk):
    i = pl.program_id(1)
    tr, d = x_ref.shape
    gw = d // len(POOL_WINDOWS)
    x = x_ref[...]
    lb = lb_ref[...]
    inv = lax.rsqrt(jnp.mean(x * x, axis=-1, keepdims=True) + EPS)
    inv_lb = lax.rsqrt(jnp.mean(lb * lb, axis=-1, keepdims=True) + EPS)
    row = i * tr - LOOKBACK + lax.broadcasted_iota(jnp.int32, (tr + LOOKBACK, 1), 0)
    valid = row >= PAD
    pos = (row[LOOKBACK:] - (PAD - 1)).astype(F32)
    for gi, win in enumerate(POOL_WINDOWS):
        div = jnp.where(valid[LOOKBACK:], jnp.minimum(pos, float(win)), 1.0)
        for cb in range(gw // col_blk):
            c0 = gi * gw + cb * col_blk
            gv = g_ref[:, c0:c0 + col_blk]
            xe = jnp.concatenate([lb[:, c0:c0 + col_blk] * inv_lb * gv, x[:, c0:c0 + col_blk] * inv * gv], axis=0)
            xe = jnp.where(valid, xe, 0.0)
            s = xe
            sh = 1
            while sh < win:
                s = s + pltpu.roll(s, sh, 0)
                sh *= 2
            o_ref[:, c0:c0 + col_blk] = (s[LOOKBACK:] / div - xe[LOOKBACK:]).astype(o_ref.dtype)


def _pool(h3, g):
    b, lp, d = h3.shape
    tr = 3 * CHUNK
    assert lp % tr == 0
    per = tr // LOOKBACK
    return pl.pallas_call(
        functools.partial(_pool_kernel, col_blk=256),
        grid=(b, lp // tr),
        in_specs=[pl.BlockSpec((None, LOOKBACK, d), lambda bi, i: (bi, jnp.maximum(i * per - 1, 0), 0)),
                  pl.BlockSpec((None, tr, d), lambda bi, i: (bi, i, 0)),
                  pl.BlockSpec((1, d), lambda bi, i: (0, 0))],
        out_specs=pl.BlockSpec((None, tr, d), lambda bi, i: (bi, i, 0)),
        out_shape=jax.ShapeDtypeStruct((b, lp, d), BF16),
        compiler_params=_params(8 * tr * d * 4, 2),
        name="pool",
    )(h3, h3, g.reshape(1, d))


def kernel(x, meta_tokens, mix_norm, dn_w_in, dn_conv_w, dn_a_log, dn_dt_bias, dn_out_norm, dn_w_out, pool_w,
           pool_scale, mlp_norm, w_up, w_down, final_norm):
    b, seq, d = x.shape
    nv = dn_a_log.shape[1]
    val_dim = nv * DV
    key_dim = (dn_conv_w.shape[2] - val_dim) // 2
    nk_heads = key_dim // DK
    qkvz = 2 * key_dim + 2 * val_dim
    d_ff = w_up.shape[2]
    depth = mix_norm.shape[0]
    lp = PAD + N_META + seq
    m = b * lp
    tm = m // 8
    tm_stream = m // 4
    tm_chunk = m // 6
    assert lp % CHUNK == 0 and tm % 16 == 0 and tm_stream % 16 == 0 and tm_chunk % 16 == 0
    assert dn_w_in.shape[2] == qkvz + 2 * nv

    meta = jnp.broadcast_to(meta_tokens.astype(x.dtype)[None], (b, N_META, d))
    h = jnp.concatenate([jnp.zeros((b, PAD, d), x.dtype), meta, x], axis=1).reshape(m, d)

    for i in range(depth):
        j = i // 2
        if i % 2 == 0:
            hn = _rmsnorm(h, mix_norm[i], BF16)
            proj = _matmul_stream(hn, dn_w_in, j, n=qkvz, tn=512, tm=tm_stream, out_dtype=BF16)
            ba = _matmul_stream(hn, dn_w_in, j, n=2 * nv, n_off=qkvz, tn=2 * nv, tm=tm_stream, out_dtype=F32)
            gcb = _gates(ba.reshape(b, lp, 2 * nv), dn_a_log[j], dn_dt_bias[j])
            og = _delta_rule(proj.reshape(b, lp, qkvz), dn_conv_w, j, gcb, _gate_rows(gcb, nv), dn_out_norm[j],
                             nk_heads=nk_heads, nv=nv)
            h = _matmul_chunked(og.reshape(m, val_dim), dn_w_out, j, h, n=d, tn=512, tm=tm_chunk, kc=d)
        else:
            pooled = _pool(h.reshape(b, lp, d), mix_norm[i])
            h = _pool_matmul(pooled.reshape(m, d), pool_w, j, pool_scale[j], h, tm=tm)
        hn = _rmsnorm(h, mlp_norm[i], BF16)
        act = _matmul_stream(hn, w_up, i, n=d_ff, tn=512, tm=tm_stream, out_dtype=BF16, relu2=True)
        h = _matmul_chunked(act, w_down, i, h, n=d, tn=512, tm=tm_chunk, kc=d)

    return _final_norm(h.reshape(b, lp, d), final_norm, seq)
```

```python
import functools

import jax
import jax.numpy as jnp
from jax import lax
from jax.experimental import pallas as pl
from jax.experimental.pallas import tpu as pltpu

F32 = jnp.float32
BF16 = jnp.bfloat16

N_META = 16
CHUNK = 64
DK = 128
DV = 128
CONV_K = 4
POOL_WINDOWS = (2, 4, 8, 16)
EPS = 1e-6
PAD = (-N_META) % CHUNK
LOOKBACK = 16
HEADS_PER_KEY = 2
PAIR_W = HEADS_PER_KEY * CHUNK

V7X_VMEM_BYTES = 64 * 1024 * 1024
MIB = 1024 * 1024


def _params(vmem_bytes, n_axes):
    limit = min(int(vmem_bytes) + 4 * MIB, V7X_VMEM_BYTES - 6 * MIB)
    return pltpu.CompilerParams(dimension_semantics=("arbitrary",) * n_axes, vmem_limit_bytes=limit)


def _rmsnorm_kernel(x_ref, g_ref, o_ref):
    x = x_ref[...]
    ms = jnp.mean(x * x, axis=-1, keepdims=True)
    o_ref[...] = (x * lax.rsqrt(ms + EPS) * g_ref[...]).astype(o_ref.dtype)


def _rmsnorm(x, g, out_dtype, tr=256):
    m, d = x.shape
    return pl.pallas_call(
        _rmsnorm_kernel,
        grid=(m // tr,),
        in_specs=[pl.BlockSpec((tr, d), lambda i: (i, 0)), pl.BlockSpec((1, d), lambda i: (0, 0))],
        out_specs=pl.BlockSpec((tr, d), lambda i: (i, 0)),
        out_shape=jax.ShapeDtypeStruct((m, d), out_dtype),
        compiler_params=_params(6 * tr * d * 4, 1),
        name="rmsnorm",
    )(x, g.reshape(1, d))


def _final_norm_kernel(main_ref, next_ref, g_ref, o_ref):
    nb = o_ref.shape[0]
    g = g_ref[...]
    for c in range(nb):
        x = main_ref[c + 1] if c + 1 < nb else next_ref[0]
        ms = jnp.mean(x * x, axis=-1, keepdims=True)
        o_ref[c] = (x * lax.rsqrt(ms + EPS) * g).astype(o_ref.dtype)


def _final_norm(h3, g, seq, nb=8):
    b, lp, d = h3.shape
    assert lp - seq == CHUNK and seq % (nb * CHUNK) == 0
    h4 = h3.reshape(b, lp // CHUNK, CHUNK, d)
    out = pl.pallas_call(
        _final_norm_kernel,
        grid=(b, seq // (nb * CHUNK)),
        in_specs=[pl.BlockSpec((None, nb, CHUNK, d), lambda bi, i: (bi, i, 0, 0)),
                  pl.BlockSpec((None, 1, CHUNK, d), lambda bi, i: (bi, (i + 1) * nb, 0, 0)),
                  pl.BlockSpec((1, d), lambda bi, i: (0, 0))],
        out_specs=pl.BlockSpec((None, nb, CHUNK, d), lambda bi, i: (bi, i, 0, 0)),
        out_shape=jax.ShapeDtypeStruct((b, seq // CHUNK, CHUNK, d), h3.dtype),
        compiler_params=_params((4 * nb + 4) * CHUNK * d * 4, 2),
        name="final_norm",
    )(h4, h4, g.reshape(1, d))
    return out.reshape(b, seq, d)


def _pool_mm_kernel(a_ref, w_ref, res_ref, scale_ref, o_ref, wb_ref):
    @pl.when(pl.program_id(1) == 0)
    def _cast_weights():
        wb_ref[...] = w_ref[...].astype(BF16)

    y = jnp.dot(a_ref[...], wb_ref[...], preferred_element_type=F32)
    o_ref[...] = (res_ref[...] + y * scale_ref[...]).astype(o_ref.dtype)


def _mm_stream_kernel(*refs, relu2, has_prev):
    if has_prev:
        a_ref, w_ref, prev_ref, o_ref = refs
    else:
        a_ref, w_ref, o_ref = refs
    y = jnp.dot(a_ref[...], w_ref[...].astype(BF16), preferred_element_type=F32)
    if relu2:
        r = jnp.maximum(y, 0.0)
        y = r * r
    if has_prev:
        y = prev_ref[...] + y
    o_ref[...] = y.astype(o_ref.dtype)


def _matmul_stream(a, w, layer, *, n, tn, tm, out_dtype, n_off=0, relu2=False, prev=None, kc=None, k_blk=0,
                   a_buffers=1):
    m, kdim = a.shape
    kc = kdim if kc is None else kc
    assert m % tm == 0 and n % tn == 0 and n_off % tn == 0 and kdim % kc == 0
    joff = n_off // tn
    osz = jnp.dtype(out_dtype).itemsize
    vmem = (2 * kc * tn * 4 + kc * tn * 2 + a_buffers * tm * kc * 2 + 2 * tm * tn * osz + tm * tn * 4
            + (2 * tm * tn * 4 if prev is not None else 0))
    in_specs = [pl.BlockSpec((tm, kc), lambda i, j: (i, k_blk), pipeline_mode=pl.Buffered(a_buffers)),
                pl.BlockSpec((None, kc, tn), lambda i, j: (layer, k_blk, j + joff))]
    args = [a, w]
    if prev is not None:
        in_specs.append(pl.BlockSpec((tm, tn), lambda i, j: (i, j)))
        args.append(prev)
    return pl.pallas_call(
        functools.partial(_mm_stream_kernel, relu2=relu2, has_prev=prev is not None),
        grid=(m // tm, n // tn),
        in_specs=in_specs,
        out_specs=pl.BlockSpec((tm, tn), lambda i, j: (i, j)),
        out_shape=jax.ShapeDtypeStruct((m, n), out_dtype),
        compiler_params=_params(vmem, 2),
        name="matmul_stream",
    )(*args)


def _matmul_chunked(a, w, layer, res, *, n, tn, tm, kc):
    for k_blk in range(a.shape[1] // kc):
        res = _matmul_stream(a, w, layer, n=n, tn=tn, tm=tm, out_dtype=res.dtype, prev=res, kc=kc, k_blk=k_blk,
                             a_buffers=2)
    return res


def _pool_matmul(pooled, pool_w, layer, scale, res, *, tm):
    m, d = pooled.shape
    n_groups, gw = pool_w.shape[1], pool_w.shape[2]
    vmem = 2 * gw * gw * 4 + gw * gw * 2 + 2 * tm * gw * 2 + 6 * tm * gw * 4
    return pl.pallas_call(
        _pool_mm_kernel,
        grid=(n_groups, m // tm),
        in_specs=[pl.BlockSpec((tm, gw), lambda j, i: (i, j)),
                  pl.BlockSpec((None, None, gw, gw), lambda j, i: (layer, j, 0, 0)),
                  pl.BlockSpec((tm, gw), lambda j, i: (i, j)),
                  pl.BlockSpec((1, gw), lambda j, i: (0, j))],
        out_specs=pl.BlockSpec((tm, gw), lambda j, i: (i, j)),
        out_shape=jax.ShapeDtypeStruct((m, d), res.dtype),
        scratch_shapes=[pltpu.VMEM((gw, gw), BF16)],
        compiler_params=_params(vmem, 2),
        name="pool_matmul",
    )(pooled, pool_w, res, scale.reshape(1, d))


def _gate_kernel(ba_ref, alog_ref, dtb_ref, o_ref, *, nv):
    n_chunks = ba_ref.shape[0] // CHUNK
    lane = lax.broadcasted_iota(jnp.int32, (CHUNK, 2 * nv), 1)
    pos = lax.broadcasted_iota(jnp.int32, (CHUNK, 2 * nv), 0)
    neg_a = -jnp.exp(alog_ref[...])
    dtb = dtb_ref[...]

    def body(c, carry):
        rows = pl.ds(pl.multiple_of(c * CHUNK, CHUNK), CHUNK)
        x = ba_ref[rows, :]
        beta = jax.nn.sigmoid(x)
        xa = x + dtb
        g = neg_a * (jnp.maximum(xa, 0.0) + jnp.log1p(jnp.exp(-jnp.abs(xa))))
        val = jnp.where(lane < nv, beta, g)
        val = jnp.where(jnp.logical_or(c > 0, pos >= PAD), val, 0.0)
        cs = val
        s = 1
        while s < CHUNK:
            cs = cs + jnp.where(pos >= s, pltpu.roll(cs, s, 0), 0.0)
            s *= 2
        o_ref[rows, :] = jnp.where(lane < nv, val, cs)
        return carry

    lax.fori_loop(0, n_chunks, body, 0)


def _gates(ba3, a_log, dt_bias):
    b, lp, w = ba3.shape
    nv = w // 2
    zeros = jnp.zeros((nv,), F32)
    alog = jnp.concatenate([zeros, a_log.astype(F32)]).reshape(1, w)
    dtb = jnp.concatenate([zeros, dt_bias.astype(F32)]).reshape(1, w)
    return pl.pallas_call(
        functools.partial(_gate_kernel, nv=nv),
        grid=(b,),
        in_specs=[pl.BlockSpec((None, lp, w), lambda bi: (bi, 0, 0)),
                  pl.BlockSpec((1, w), lambda bi: (0, 0)),
                  pl.BlockSpec((1, w), lambda bi: (0, 0))],
        out_specs=pl.BlockSpec((None, lp, w), lambda bi: (bi, 0, 0)),
        out_shape=jax.ShapeDtypeStruct((b, lp, w), F32),
        compiler_params=_params(8 * lp * w * 4, 1),
        name="delta_gates",
    )(ba3, alog, dtb)


def _gate_rows(gcb, nv):
    b, lp, _ = gcb.shape
    nkh = nv // HEADS_PER_KEY
    nc = lp // CHUNK
    gcr = gcb[:, :, nv:].reshape(b, nc, CHUNK, nkh, HEADS_PER_KEY).transpose(0, 3, 1, 4, 2)
    gl = gcr[..., CHUNK - 1:]
    return jnp.concatenate(
        [gcr.reshape(b, nkh, nc, PAIR_W),
         jnp.broadcast_to(gl, (b, nkh, nc, HEADS_PER_KEY, CHUNK)).reshape(b, nkh, nc, PAIR_W),
         jnp.broadcast_to(gl, (b, nkh, nc, HEADS_PER_KEY, DV)).reshape(b, nkh, nc, HEADS_PER_KEY * DV)], axis=-1)


def _block_diag(a, b):
    z = jnp.zeros_like(a)
    return jnp.concatenate([jnp.concatenate([a, z], axis=1), jnp.concatenate([z, b], axis=1)], axis=0)


def _delta_kernel(q_ref, k_ref, v_ref, z_ref, cwq_ref, cwk_ref, cwv_ref, gcb_ref, rows_ref, onorm_ref,
                  o_ref, u_s, wq_s, qk_s, kb_s, edg_s, st_s, *, nv, keys, group):
    kp = pl.program_id(1)
    n_chunks = q_ref.shape[0] // CHUNK
    vw = HEADS_PER_KEY * DV
    rowi = lax.broadcasted_iota(jnp.int32, (CHUNK, PAIR_W), 0)
    lanei = lax.broadcasted_iota(jnp.int32, (CHUNK, PAIR_W), 1)
    coli = lanei & (CHUNK - 1)
    left = lanei < CHUNK
    lower = rowi >= coli
    strict = rowi > coli
    eye = jnp.where(rowi == coli, 1.0, 0.0).astype(F32)
    lane_g = lax.broadcasted_iota(jnp.int32, (CHUNK, 2 * nv), 1)
    nt_dims = (((1,), (1,)), ((), ()))
    tn_dims = (((0,), (0,)), ((), ()))

    def mm_pair(x, y):
        ybd = jnp.concatenate([jnp.where(left, y, 0.0), jnp.where(left, 0.0, y)], axis=0).astype(BF16)
        return jnp.dot(x.astype(BF16), ybd, preferred_element_type=F32)

    def l2n(t, scale=1.0):
        return t * (lax.rsqrt(jnp.sum(t * t, axis=-1, keepdims=True) + EPS) * scale)

    def conv_silu(ref, cw_ref, rows, lb_rows, c0, width):
        x = jnp.concatenate([ref[lb_rows, c0:c0 + width], ref[rows, c0:c0 + width]], axis=0).astype(F32)
        w = cw_ref[:, c0:c0 + width]
        y = x[LOOKBACK:] * w[CONV_K - 1:CONV_K]
        for j in range(1, CONV_K):
            y = y + x[LOOKBACK - j:LOOKBACK - j + CHUNK] * w[CONV_K - 1 - j:CONV_K - j]
        return y * jax.nn.sigmoid(y)

    LOCAL_STAGES = 10

    def slot_row(g, ci):
        return pl.multiple_of(((g % 2) * group + ci) * CHUNK, CHUNK)

    def local_stages(g):
        chains = []
        for ci in range(group):
            c = g * group + ci
            r0 = pl.multiple_of(c * CHUNK, CHUNK)
            lb_rows = pl.ds(pl.multiple_of(jnp.maximum(r0 - LOOKBACK, 0), LOOKBACK), LOOKBACK)
            for e in range(keys):
                chains.append(dict(c=c, e=e, rows=pl.ds(r0, CHUNK), lb_rows=lb_rows, s0=slot_row(g, ci)))

        for ch in chains:
            e, rows, lb_rows = ch["e"], ch["rows"], ch["lb_rows"]
            q = l2n(conv_silu(q_ref, cwq_ref, rows, lb_rows, e * DK, DK), DK ** -0.5)
            k = l2n(conv_silu(k_ref, cwk_ref, rows, lb_rows, e * DK, DK))
            qb = q.astype(BF16)
            kb = k.astype(BF16)
            qkk = lax.dot_general(jnp.concatenate([qb, kb], axis=0), jnp.concatenate([kb, kb], axis=0),
                                  nt_dims, preferred_element_type=F32)
            ch.update(q=q, k=k, kb=kb, qkk=qkk)
        yield

        for ch in chains:
            gcb = gcb_ref[ch["rows"], :]

            def column(idx, gcb=gcb):
                return jnp.sum(jnp.where(lane_g == idx, gcb, 0.0), axis=1, keepdims=True)

            h0 = HEADS_PER_KEY * (keys * kp + ch["e"])
            beta_a, beta_b = column(h0), column(h0 + 1)
            gc_a, gc_b = column(nv + h0), column(nv + h0 + 1)
            gc = jnp.where(left, gc_a, gc_b)
            rr = rows_ref[ch["e"], pl.ds(ch["c"], 1), :]
            decay = jnp.where(lower, jnp.exp(jnp.where(lower, gc - rr[:, :PAIR_W], 0.0)), 0.0)
            n = -jnp.where(strict, (jnp.where(left, beta_a, beta_b) * ch["qkk"][CHUNK:]) * decay, 0.0)
            ch.update(beta_a=beta_a, beta_b=beta_b, eg_a=jnp.exp(gc_a), eg_b=jnp.exp(gc_b), decay=decay,
                      edg=jnp.exp(rr[:, PAIR_W:2 * PAIR_W] - gc), t=eye + n, p=n)

        assert CHUNK == 64

        def square_and_extend(p, x):
            both = mm_pair(jnp.concatenate([p, x], axis=0), p)
            return both[:CHUNK], x + both[CHUNK:]

        for ch in chains:
            ch["p2"] = mm_pair(ch["p"], ch["p"])
        yield
        for ch in chains:
            ch["p4"], ch["a1"] = square_and_extend(ch["p2"], ch["t"])
        yield
        for ch in chains:
            ch["p8"] = mm_pair(ch["p4"], ch["p4"])
        yield
        for ch in chains:
            ch["p16"], ch["a2"] = square_and_extend(ch["p8"], eye + ch["p4"])
        yield
        for ch in chains:
            ch["p32"] = mm_pair(ch["p16"], ch["p16"])
        for ch in chains:
            ch["a12"] = mm_pair(ch["a1"], ch["a2"])
        yield
        for ch in chains:
            x = eye + ch["p16"]
            ch["a3"] = x + mm_pair(x, ch["p32"])
        yield
        for ch in chains:
            ch["t"] = mm_pair(ch["a12"], ch["a3"])
        yield

        for ch in chains:
            e, rows = ch["e"], ch["rows"]
            v = conv_silu(v_ref, cwv_ref, rows, ch["lb_rows"], e * vw, vw)
            k = ch["k"]
            rhs_a = jnp.concatenate([v[:, :DV] * ch["beta_a"], k * (ch["beta_a"] * ch["eg_a"])], axis=1)
            rhs_b = jnp.concatenate([v[:, DV:] * ch["beta_b"], k * (ch["beta_b"] * ch["eg_b"])], axis=1)
            ch["uw"] = jnp.dot(ch["t"].astype(BF16), _block_diag(rhs_a.astype(BF16), rhs_b.astype(BF16)),
                               preferred_element_type=F32)
        yield

        for ch in chains:
            e, uw, q = ch["e"], ch["uw"], ch["q"]
            rows = pl.ds(ch["s0"], CHUNK)
            wq0 = pl.multiple_of(ch["s0"] * 2, 2 * CHUNK)
            u_s[e, rows, :] = jnp.concatenate([uw[:, :DV], uw[:, 2 * DV:3 * DV]], axis=1)
            wq_s[e, pl.ds(wq0, CHUNK), :] = jnp.concatenate([uw[:, DV:2 * DV], uw[:, 3 * DV:]], axis=1).astype(BF16)
            wq_s[e, pl.ds(wq0 + CHUNK, CHUNK), :] = jnp.concatenate([q * ch["eg_a"], q * ch["eg_b"]],
                                                                     axis=1).astype(BF16)
            qk_s[e, rows, :] = (ch["qkk"][:CHUNK] * ch["decay"]).astype(BF16)
            kb_s[e, rows, :] = ch["kb"]
            edg_s[e, rows, :] = ch["edg"]

    onorm = onorm_ref[...]

    def scan_stages(g):
        states = [st_s[e] for e in range(keys)]
        for ci in range(group):
            c = g * group + ci
            io_rows = pl.ds(pl.multiple_of(c * CHUNK, CHUNK), CHUNK)
            rows = pl.ds(slot_row(g, ci), CHUNK)
            wq_rows = pl.ds(pl.multiple_of(slot_row(g, ci) * 2, 2 * CHUNK), 2 * CHUNK)
            r1s, v_news, outs = [], [], []
            for e in range(keys):
                sb = states[e].astype(BF16)
                r1s.append(jnp.dot(wq_s[e, wq_rows, :], _block_diag(sb[:, :DV], sb[:, DV:]),
                                   preferred_element_type=F32))
            yield
            for e in range(keys):
                v_news.append(u_s[e, rows, :] - r1s[e][:CHUNK])
            for e in range(keys):
                v_new = v_news[e]
                edg = edg_s[e, rows, :]
                edg_sw = pltpu.roll(edg, CHUNK, 1)
                dv = jnp.concatenate([v_new[:, :DV] * jnp.where(left, edg, edg_sw),
                                      v_new[:, DV:] * jnp.where(left, edg_sw, edg)], axis=1).astype(BF16)
                egl = jnp.exp(rows_ref[e, pl.ds(c, 1), 2 * PAIR_W:])
                states[e] = states[e] * egl + lax.dot_general(kb_s[e, rows, :], dv, tn_dims,
                                                              preferred_element_type=F32)
            for e in range(keys):
                vb = v_news[e].astype(BF16)
                outs.append(r1s[e][CHUNK:] + jnp.dot(qk_s[e, rows, :], _block_diag(vb[:, :DV], vb[:, DV:]),
                                                     preferred_element_type=F32))
            yield
            for e in range(keys):
                for j in range(HEADS_PER_KEY):
                    oj = outs[e][:, j * DV:(j + 1) * DV]
                    on = oj * lax.rsqrt(jnp.mean(oj * oj, axis=-1, keepdims=True) + EPS) * onorm
                    c0 = e * vw + j * DV
                    z = z_ref[io_rows, c0:c0 + DV].astype(F32)
                    o_ref[io_rows, c0:c0 + DV] = (on * (z * jax.nn.sigmoid(z))).astype(o_ref.dtype)
        for e in range(keys):
            st_s[e] = states[e]

    def run(local_gen, scan_gen):
        gens = [g for g in (local_gen, scan_gen) if g is not None]
        counts = {id(local_gen): LOCAL_STAGES, id(scan_gen): 2 * group + 1}
        order = sorted((float(s + 0.5) / counts[id(g)], n, g) for n, g in enumerate(gens)
                       for s in range(counts[id(g)]))
        for _, _, g in order:
            next(g, None)
        for g in gens:
            assert next(g, "done") == "done"

    n_groups = n_chunks // group
    st_s[...] = jnp.zeros_like(st_s)
    run(local_stages(0), None)

    def merged(g, carry):
        run(local_stages(g + 1), scan_stages(g))
        return carry

    lax.fori_loop(0, n_groups - 1, merged, 0)
    run(None, scan_stages(n_groups - 1))


def _delta_rule(proj3, conv_w, layer, gcb, rows, out_norm, *, nk_heads, nv, keys=4, group=3):
    b, lp, _ = proj3.shape
    assert nv == HEADS_PER_KEY * nk_heads and nk_heads % keys == 0 and PAIR_W == 128
    assert (lp // CHUNK) % group == 0
    kw = keys * DK
    vw = keys * HEADS_PER_KEY * DV
    key_dim = nk_heads * DK
    k_blk = key_dim // kw
    v_blk = 2 * key_dim // vw
    z_blk = (2 * key_dim + nv * DV) // vw
    n_chunks = lp // CHUNK
    pvw = HEADS_PER_KEY * DV
    sr = 2 * group * CHUNK
    scratch = [pltpu.VMEM((keys, sr, pvw), F32),
               pltpu.VMEM((keys, 2 * sr, pvw), BF16),
               pltpu.VMEM((keys, sr, PAIR_W), BF16),
               pltpu.VMEM((keys, sr, DK), BF16),
               pltpu.VMEM((keys, sr, PAIR_W), F32),
               pltpu.VMEM((keys, DK, pvw), F32)]
    vmem = (2 * lp * (2 * kw + 3 * vw) * 2 + 2 * lp * 2 * nv * 4
            + keys * sr * (pvw * 4 + 2 * pvw * 2 + PAIR_W * 2 + DK * 2 + PAIR_W * 4) + 12 * MIB)
    return pl.pallas_call(
        functools.partial(_delta_kernel, nv=nv, keys=keys, group=group),
        grid=(b, nk_heads // keys),
        in_specs=[pl.BlockSpec((None, lp, kw), lambda bi, kp: (bi, 0, kp)),
                  pl.BlockSpec((None, lp, kw), lambda bi, kp: (bi, 0, k_blk + kp)),
                  pl.BlockSpec((None, lp, vw), lambda bi, kp: (bi, 0, v_blk + kp)),
                  pl.BlockSpec((None, lp, vw), lambda bi, kp: (bi, 0, z_blk + kp)),
                  pl.BlockSpec((None, CONV_K, kw), lambda bi, kp: (layer, 0, kp)),
                  pl.BlockSpec((None, CONV_K, kw), lambda bi, kp: (layer, 0, k_blk + kp)),
                  pl.BlockSpec((None, CONV_K, vw), lambda bi, kp: (layer, 0, v_blk + kp)),
                  pl.BlockSpec((None, lp, 2 * nv), lambda bi, kp: (bi, 0, 0)),
                  pl.BlockSpec((None, keys, n_chunks, rows.shape[-1]), lambda bi, kp: (bi, kp, 0, 0)),
                  pl.BlockSpec((1, DV), lambda bi, kp: (0, 0))],
        out_specs=pl.BlockSpec((None, lp, vw), lambda bi, kp: (bi, 0, kp)),
        out_shape=jax.ShapeDtypeStruct((b, lp, nv * DV), BF16),
        scratch_shapes=scratch,
        compiler_params=_params(vmem, 2),
        name="delta_rule",
    )(proj3, proj3, proj3, proj3, conv_w, conv_w, conv_w, gcb, rows, out_norm.reshape(1, DV))


def _pool_kernel(lb_ref, x_ref, g_ref, o_ref, *, col_blk):
    i = pl.program_id(1)
    tr, d = x_ref.shape
    gw = d // len(POOL_WINDOWS)
    x = x_ref[...]
    lb = lb_ref[...]
    inv = lax.rsqrt(jnp.mean(x * x, axis=-1, keepdims=True) + EPS)
    inv_lb = lax.rsqrt(jnp.mean(lb * lb, axis=-1, keepdims=True) + EPS)
    row = i * tr - LOOKBACK + lax.broadcasted_iota(jnp.int32, (tr + LOOKBACK, 1), 0)
    valid = row >= PAD
    pos = (row[LOOKBACK:] - (PAD - 1)).astype(F32)
    for gi, win in enumerate(POOL_WINDOWS):
        div = jnp.where(valid[LOOKBACK:], jnp.minimum(pos, float(win)), 1.0)
        for cb in range(gw // col_blk):
            c0 = gi * gw + cb * col_blk
            gv = g_ref[:, c0:c0 + col_blk]
            xe = jnp.concatenate([lb[:, c0:c0 + col_blk] * inv_lb * gv, x[:, c0:c0 + col_blk] * inv * gv], axis=0)
            xe = jnp.where(valid, xe, 0.0)
            s = xe
            sh = 1
            while sh < win:
                s = s + pltpu.roll(s, sh, 0)
                sh *= 2
            o_ref[:, c0:c0 + col_blk] = (s[LOOKBACK:] / div - xe[LOOKBACK:]).astype(o_ref.dtype)


def _pool(h3, g):
    b, lp, d = h3.shape
    tr = 3 * CHUNK
    assert lp % tr == 0
    per = tr // LOOKBACK
    return pl.pallas_call(
        functools.partial(_pool_kernel, col_blk=256),
        grid=(b, lp // tr),
        in_specs=[pl.BlockSpec((None, LOOKBACK, d), lambda bi, i: (bi, jnp.maximum(i * per - 1, 0), 0)),
                  pl.BlockSpec((None, tr, d), lambda bi, i: (bi, i, 0)),
                  pl.BlockSpec((1, d), lambda bi, i: (0, 0))],
        out_specs=pl.BlockSpec((None, tr, d), lambda bi, i: (bi, i, 0)),
        out_shape=jax.ShapeDtypeStruct((b, lp, d), BF16),
        compiler_params=_params(8 * tr * d * 4, 2),
        name="pool",
    )(h3, h3, g.reshape(1, d))


def kernel(x, meta_tokens, mix_norm, dn_w_in, dn_conv_w, dn_a_log, dn_dt_bias, dn_out_norm, dn_w_out, pool_w,
           pool_scale, mlp_norm, w_up, w_down, final_norm):
    b, seq, d = x.shape
    nv = dn_a_log.shape[1]
    val_dim = nv * DV
    key_dim = (dn_conv_w.shape[2] - val_dim) // 2
    nk_heads = key_dim // DK
    qkvz = 2 * key_dim + 2 * val_dim
    d_ff = w_up.shape[2]
    depth = mix_norm.shape[0]
    lp = PAD + N_META + seq
    m = b * lp
    tm = m // 8
    tm_stream = m // 4
    tm_chunk = m // 6
    assert lp % CHUNK == 0 and tm % 16 == 0 and tm_stream % 16 == 0 and tm_chunk % 16 == 0
    assert dn_w_in.shape[2] == qkvz + 2 * nv

    meta = jnp.broadcast_to(meta_tokens.astype(x.dtype)[None], (b, N_META, d))
    h = jnp.concatenate([jnp.zeros((b, PAD, d), x.dtype), meta, x], axis=1).reshape(m, d)

    for i in range(depth):
        j = i // 2
        if i % 2 == 0:
            hn = _rmsnorm(h, mix_norm[i], BF16)
            proj = _matmul_stream(hn, dn_w_in, j, n=qkvz, tn=512, tm=tm_stream, out_dtype=BF16)
            ba = _matmul_stream(hn, dn_w_in, j, n=2 * nv, n_off=qkvz, tn=2 * nv, tm=tm_stream, out_dtype=F32)
            gcb = _gates(ba.reshape(b, lp, 2 * nv), dn_a_log[j], dn_dt_bias[j])
            og = _delta_rule(proj.reshape(b, lp, qkvz), dn_conv_w, j, gcb, _gate_rows(gcb, nv), dn_out_norm[j],
                             nk_heads=nk_heads, nv=nv)
            h = _matmul_chunked(og.reshape(m, val_dim), dn_w_out, j, h, n=d, tn=512, tm=tm_chunk, kc=d)
        else:
            pooled = _pool(h.reshape(b, lp, d), mix_norm[i])
            h = _pool_matmul(pooled.reshape(m, d), pool_w, j, pool_scale[j], h, tm=tm)
        hn = _rmsnorm(h, mlp_norm[i], BF16)
        act = _matmul_stream(hn, w_up, i, n=d_ff, tn=512, tm=tm_stream, out_dtype=BF16, relu2=True)
        h = _matmul_chunked(act, w_down, i, h, n=d, tn=512, tm=tm_chunk, kc=d)

    return _final_norm(h.reshape(b, lp, d), final_norm, seq)
```

```python
import functools

import jax
import jax.numpy as jnp
from jax import lax
from jax.experimental import pallas as pl
from jax.experimental.pallas import tpu as pltpu

F32 = jnp.float32
BF16 = jnp.bfloat16

N_META = 16
CHUNK = 64
DK = 128
DV = 128
CONV_K = 4
POOL_WINDOWS = (2, 4, 8, 16)
EPS = 1e-6
PAD = (-N_META) % CHUNK
LOOKBACK = 16
HEADS_PER_KEY = 2
PAIR_W = HEADS_PER_KEY * CHUNK

V7X_VMEM_BYTES = 64 * 1024 * 1024
MIB = 1024 * 1024


def _params(vmem_bytes, n_axes):
    limit = min(int(vmem_bytes) + 4 * MIB, V7X_VMEM_BYTES - 6 * MIB)
    return pltpu.CompilerParams(dimension_semantics=("arbitrary",) * n_axes, vmem_limit_bytes=limit)


def _rmsnorm_kernel(x_ref, g_ref, o_ref):
    x = x_ref[...]
    ms = jnp.mean(x * x, axis=-1, keepdims=True)
    o_ref[...] = (x * lax.rsqrt(ms + EPS) * g_ref[...]).astype(o_ref.dtype)


def _rmsnorm(x, g, out_dtype, tr=256):
    m, d = x.shape
    return pl.pallas_call(
        _rmsnorm_kernel,
        grid=(m // tr,),
        in_specs=[pl.BlockSpec((tr, d), lambda i: (i, 0)), pl.BlockSpec((1, d), lambda i: (0, 0))],
        out_specs=pl.BlockSpec((tr, d), lambda i: (i, 0)),
        out_shape=jax.ShapeDtypeStruct((m, d), out_dtype),
        compiler_params=_params(6 * tr * d * 4, 1),
        name="rmsnorm",
    )(x, g.reshape(1, d))


def _final_norm_kernel(main_ref, next_ref, g_ref, o_ref):
    nb = o_ref.shape[0]
    g = g_ref[...]
    for c in range(nb):
        x = main_ref[c + 1] if c + 1 < nb else next_ref[0]
        ms = jnp.mean(x * x, axis=-1, keepdims=True)
        o_ref[c] = (x * lax.rsqrt(ms + EPS) * g).astype(o_ref.dtype)


def _final_norm(h3, g, seq, nb=8):
    b, lp, d = h3.shape
    assert lp - seq == CHUNK and seq % (nb * CHUNK) == 0
    h4 = h3.reshape(b, lp // CHUNK, CHUNK, d)
    out = pl.pallas_call(
        _final_norm_kernel,
        grid=(b, seq // (nb * CHUNK)),
        in_specs=[pl.BlockSpec((None, nb, CHUNK, d), lambda bi, i: (bi, i, 0, 0)),
                  pl.BlockSpec((None, 1, CHUNK, d), lambda bi, i: (bi, (i + 1) * nb, 0, 0)),
                  pl.BlockSpec((1, d), lambda bi, i: (0, 0))],
        out_specs=pl.BlockSpec((None, nb, CHUNK, d), lambda bi, i: (bi, i, 0, 0)),
        out_shape=jax.ShapeDtypeStruct((b, seq // CHUNK, CHUNK, d), h3.dtype),
        compiler_params=_params((4 * nb + 4) * CHUNK * d * 4, 2),
        name="final_norm",
    )(h4, h4, g.reshape(1, d))
    return out.reshape(b, seq, d)


def _pool_mm_kernel(a_ref, w_ref, res_ref, scale_ref, o_ref, wb_ref):
    @pl.when(pl.program_id(1) == 0)
    def _cast_weights():
        wb_ref[...] = w_ref[...].astype(BF16)

    y = jnp.dot(a_ref[...], wb_ref[...], preferred_element_type=F32)
    o_ref[...] = (res_ref[...] + y * scale_ref[...]).astype(o_ref.dtype)


def _mm_stream_kernel(*refs, relu2, has_prev):
    if has_prev:
        a_ref, w_ref, prev_ref, o_ref = refs
    else:
        a_ref, w_ref, o_ref = refs
    y = jnp.dot(a_ref[...], w_ref[...].astype(BF16), preferred_element_type=F32)
    if relu2:
        r = jnp.maximum(y, 0.0)
        y = r * r
    if has_prev:
        y = prev_ref[...] + y
    o_ref[...] = y.astype(o_ref.dtype)


def _matmul_stream(a, w, layer, *, n, tn, tm, out_dtype, n_off=0, relu2=False, prev=None, kc=None, k_blk=0,
                   a_buffers=1):
    m, kdim = a.shape
    kc = kdim if kc is None else kc
    assert m % tm == 0 and n % tn == 0 and n_off % tn == 0 and kdim % kc == 0
    joff = n_off // tn
    osz = jnp.dtype(out_dtype).itemsize
    vmem = (2 * kc * tn * 4 + kc * tn * 2 + a_buffers * tm * kc * 2 + 2 * tm * tn * osz + tm * tn * 4
            + (2 * tm * tn * 4 if prev is not None else 0))
    in_specs = [pl.BlockSpec((tm, kc), lambda i, j: (i, k_blk), pipeline_mode=pl.Buffered(a_buffers)),
                pl.BlockSpec((None, kc, tn), lambda i, j: (layer, k_blk, j + joff))]
    args = [a, w]
    if prev is not None:
        in_specs.append(pl.BlockSpec((tm, tn), lambda i, j: (i, j)))
        args.append(prev)
    return pl.pallas_call(
        functools.partial(_mm_stream_kernel, relu2=relu2, has_prev=prev is not None),
        grid=(m // tm, n // tn),
        in_specs=in_specs,
        out_specs=pl.BlockSpec((tm, tn), lambda i, j: (i, j)),
        out_shape=jax.ShapeDtypeStruct((m, n), out_dtype),
        compiler_params=_params(vmem, 2),
        name="matmul_stream",
    )(*args)


def _matmul_chunked(a, w, layer, res, *, n, tn, tm, kc):
    for k_blk in range(a.shape[1] // kc):
        res = _matmul_stream(a, w, layer, n=n, tn=tn, tm=tm, out_dtype=res.dtype, prev=res, kc=kc, k_blk=k_blk,
                             a_buffers=2)
    return res


def _pool_matmul(pooled, pool_w, layer, scale, res, *, tm):
    m, d = pooled.shape
    n_groups, gw = pool_w.shape[1], pool_w.shape[2]
    vmem = 2 * gw * gw * 4 + gw * gw * 2 + 2 * tm * gw * 2 + 6 * tm * gw * 4
    return pl.pallas_call(
        _pool_mm_kernel,
        grid=(n_groups, m // tm),
        in_specs=[pl.BlockSpec((tm, gw), lambda j, i: (i, j)),
                  pl.BlockSpec((None, None, gw, gw), lambda j, i: (layer, j, 0, 0)),
                  pl.BlockSpec((tm, gw), lambda j, i: (i, j)),
                  pl.BlockSpec((1, gw), lambda j, i: (0, j))],
        out_specs=pl.BlockSpec((tm, gw), lambda j, i: (i, j)),
        out_shape=jax.ShapeDtypeStruct((m, d), res.dtype),
        scratch_shapes=[pltpu.VMEM((gw, gw), BF16)],
        compiler_params=_params(vmem, 2),
        name="pool_matmul",
    )(pooled, pool_w, res, scale.reshape(1, d))


def _gate_kernel(ba_ref, alog_ref, dtb_ref, o_ref, *, nv):
    n_chunks = ba_ref.shape[0] // CHUNK
    lane = lax.broadcasted_iota(jnp.int32, (CHUNK, 2 * nv), 1)
    pos = lax.broadcasted_iota(jnp.int32, (CHUNK, 2 * nv), 0)
    neg_a = -jnp.exp(alog_ref[...])
    dtb = dtb_ref[...]

    def body(c, carry):
        rows = pl.ds(pl.multiple_of(c * CHUNK, CHUNK), CHUNK)
        x = ba_ref[rows, :]
        beta = jax.nn.sigmoid(x)
        xa = x + dtb
        g = neg_a * (jnp.maximum(xa, 0.0) + jnp.log1p(jnp.exp(-jnp.abs(xa))))
        val = jnp.where(lane < nv, beta, g)
        val = jnp.where(jnp.logical_or(c > 0, pos >= PAD), val, 0.0)
        cs = val
        s = 1
        while s < CHUNK:
            cs = cs + jnp.where(pos >= s, pltpu.roll(cs, s, 0), 0.0)
            s *= 2
        o_ref[rows, :] = jnp.where(lane < nv, val, cs)
        return carry

    lax.fori_loop(0, n_chunks, body, 0)


def _gates(ba3, a_log, dt_bias):
    b, lp, w = ba3.shape
    nv = w // 2
    zeros = jnp.zeros((nv,), F32)
    alog = jnp.concatenate([zeros, a_log.astype(F32)]).reshape(1, w)
    dtb = jnp.concatenate([zeros, dt_bias.astype(F32)]).reshape(1, w)
    return pl.pallas_call(
        functools.partial(_gate_kernel, nv=nv),
        grid=(b,),
        in_specs=[pl.BlockSpec((None, lp, w), lambda bi: (bi, 0, 0)),
                  pl.BlockSpec((1, w), lambda bi: (0, 0)),
                  pl.BlockSpec((1, w), lambda bi: (0, 0))],
        out_specs=pl.BlockSpec((None, lp, w), lambda bi: (bi, 0, 0)),
        out_shape=jax.ShapeDtypeStruct((b, lp, w), F32),
        compiler_params=_params(8 * lp * w * 4, 1),
        name="delta_gates",
    )(ba3, alog, dtb)


def _gate_rows(gcb, nv):
    b, lp, _ = gcb.shape
    nkh = nv // HEADS_PER_KEY
    nc = lp // CHUNK
    gcr = gcb[:, :, nv:].reshape(b, nc, CHUNK, nkh, HEADS_PER_KEY).transpose(0, 3, 1, 4, 2)
    gl = gcr[..., CHUNK - 1:]
    return jnp.concatenate(
        [gcr.reshape(b, nkh, nc, PAIR_W),
         jnp.broadcast_to(gl, (b, nkh, nc, HEADS_PER_KEY, CHUNK)).reshape(b, nkh, nc, PAIR_W),
         jnp.broadcast_to(gl, (b, nkh, nc, HEADS_PER_KEY, DV)).reshape(b, nkh, nc, HEADS_PER_KEY * DV)], axis=-1)


def _block_diag(a, b):
    z = jnp.zeros_like(a)
    return jnp.concatenate([jnp.concatenate([a, z], axis=1), jnp.concatenate([z, b], axis=1)], axis=0)


def _delta_kernel(q_ref, k_ref, v_ref, z_ref, cwq_ref, cwk_ref, cwv_ref, gcb_ref, rows_ref, onorm_ref,
                  o_ref, u_s, wq_s, qk_s, kb_s, edg_s, st_s, *, nv, keys, group):
    kp = pl.program_id(1)
    n_chunks = q_ref.shape[0] // CHUNK
    vw = HEADS_PER_KEY * DV
    rowi = lax.broadcasted_iota(jnp.int32, (CHUNK, PAIR_W), 0)
    lanei = lax.broadcasted_iota(jnp.int32, (CHUNK, PAIR_W), 1)
    coli = lanei & (CHUNK - 1)
    left = lanei < CHUNK
    lower = rowi >= coli
    strict = rowi > coli
    eye = jnp.where(rowi == coli, 1.0, 0.0).astype(F32)
    lane_g = lax.broadcasted_iota(jnp.int32, (CHUNK, 2 * nv), 1)
    nt_dims = (((1,), (1,)), ((), ()))
    tn_dims = (((0,), (0,)), ((), ()))

    def mm_pair(x, y):
        ybd = jnp.concatenate([jnp.where(left, y, 0.0), jnp.where(left, 0.0, y)], axis=0).astype(BF16)
        return jnp.dot(x.astype(BF16), ybd, preferred_element_type=F32)

    def l2n(t, scale=1.0):
        return t * (lax.rsqrt(jnp.sum(t * t, axis=-1, keepdims=True) + EPS) * scale)

    def conv_silu(ref, cw_ref, rows, lb_rows, c0, width):
        x = jnp.concatenate([ref[lb_rows, c0:c0 + width], ref[rows, c0:c0 + width]], axis=0).astype(F32)
        w = cw_ref[:, c0:c0 + width]
        y = x[LOOKBACK:] * w[CONV_K - 1:CONV_K]
        for j in range(1, CONV_K):
            y = y + x[LOOKBACK - j:LOOKBACK - j + CHUNK] * w[CONV_K - 1 - j:CONV_K - j]
        return y * jax.nn.sigmoid(y)

    LOCAL_STAGES = 10

    def slot_row(g, ci):
        return pl.multiple_of(((g % 2) * group + ci) * CHUNK, CHUNK)

    def local_stages(g):
        chains = []
        for ci in range(group):
            c = g * group + ci
            r0 = pl.multiple_of(c * CHUNK, CHUNK)
            lb_rows = pl.ds(pl.multiple_of(jnp.maximum(r0 - LOOKBACK, 0), LOOKBACK), LOOKBACK)
            for e in range(keys):
                chains.append(dict(c=c, e=e, rows=pl.ds(r0, CHUNK), lb_rows=lb_rows, s0=slot_row(g, ci)))

        for ch in chains:
            e, rows, lb_rows = ch["e"], ch["rows"], ch["lb_rows"]
            q = l2n(conv_silu(q_ref, cwq_ref, rows, lb_rows, e * DK, DK), DK ** -0.5)
            k = l2n(conv_silu(k_ref, cwk_ref, rows, lb_rows, e * DK, DK))
            qb = q.astype(BF16)
            kb = k.astype(BF16)
            qkk = lax.dot_general(jnp.concatenate([qb, kb], axis=0), jnp.concatenate([kb, kb], axis=0),
                                  nt_dims, preferred_element_type=F32)
            ch.update(q=q, k=k, kb=kb, qkk=qkk)
        yield

        for ch in chains:
            gcb = gcb_ref[ch["rows"], :]

            def column(idx, gcb=gcb):
                return jnp.sum(jnp.where(lane_g == idx, gcb, 0.0), axis=1, keepdims=True)

            h0 = HEADS_PER_KEY * (keys * kp + ch["e"])
            beta_a, beta_b = column(h0), column(h0 + 1)
            gc_a, gc_b = column(nv + h0), column(nv + h0 + 1)
            gc = jnp.where(left, gc_a, gc_b)
            rr = rows_ref[ch["e"], pl.ds(ch["c"], 1), :]
            decay = jnp.where(lower, jnp.exp(jnp.where(lower, gc - rr[:, :PAIR_W], 0.0)), 0.0)
            n = -jnp.where(strict, (jnp.where(left, beta_a, beta_b) * ch["qkk"][CHUNK:]) * decay, 0.0)
            ch.update(beta_a=beta_a, beta_b=beta_b, eg_a=jnp.exp(gc_a), eg_b=jnp.exp(gc_b), decay=decay,
                      edg=jnp.exp(rr[:, PAIR_W:2 * PAIR_W] - gc), t=eye + n, p=n)

        assert CHUNK == 64

        def square_and_extend(p, x):
            both = mm_pair(jnp.concatenate([p, x], axis=0), p)
            return both[:CHUNK], x + both[CHUNK:]

        for ch in chains:
            ch["p2"] = mm_pair(ch["p"], ch["p"])
        yield
        for ch in chains:
            ch["p4"], ch["a1"] = square_and_extend(ch["p2"], ch["t"])
        yield
        for ch in chains:
            ch["p8"] = mm_pair(ch["p4"], ch["p4"])
        yield
        for ch in chains:
            ch["p16"], ch["a2"] = square_and_extend(ch["p8"], eye + ch["p4"])
        yield
        for ch in chains:
            ch["p32"] = mm_pair(ch["p16"], ch["p16"])
        for ch in chains:
            ch["a12"] = mm_pair(ch["a1"], ch["a2"])
        yield
        for ch in chains:
            x = eye + ch["p16"]
            ch["a3"] = x + mm_pair(x, ch["p32"])
        yield
        for ch in chains:
            ch["t"] = mm_pair(ch["a12"], ch["a3"])
        yield

        for ch in chains:
            e, rows = ch["e"], ch["rows"]
            v = conv_silu(v_ref, cwv_ref, rows, ch["lb_rows"], e * vw, vw)
            k = ch["k"]
            rhs_a = jnp.concatenate([v[:, :DV] * ch["beta_a"], k * (ch["beta_a"] * ch["eg_a"])], axis=1)
            rhs_b = jnp.concatenate([v[:, DV:] * ch["beta_b"], k * (ch["beta_b"] * ch["eg_b"])], axis=1)
            ch["uw"] = jnp.dot(ch["t"].astype(BF16), _block_diag(rhs_a.astype(BF16), rhs_b.astype(BF16)),
                               preferred_element_type=F32)
        yield

        for ch in chains:
            e, uw, q = ch["e"], ch["uw"], ch["q"]
            rows = pl.ds(ch["s0"], CHUNK)
            wq0 = pl.multiple_of(ch["s0"] * 2, 2 * CHUNK)
            u_s[e, rows, :] = jnp.concatenate([uw[:, :DV], uw[:, 2 * DV:3 * DV]], axis=1)
            wq_s[e, pl.ds(wq0, CHUNK), :] = jnp.concatenate([uw[:, DV:2 * DV], uw[:, 3 * DV:]], axis=1).astype(BF16)
            wq_s[e, pl.ds(wq0 + CHUNK, CHUNK), :] = jnp.concatenate([q * ch["eg_a"], q * ch["eg_b"]],
                                                                     axis=1).astype(BF16)
            qk_s[e, rows, :] = (ch["qkk"][:CHUNK] * ch["decay"]).astype(BF16)
            kb_s[e, rows, :] = ch["kb"]
            edg_s[e, rows, :] = ch["edg"]

    onorm = onorm_ref[...]

    def scan_stages(g):
        states = [st_s[e] for e in range(keys)]
        for ci in range(group):
            c = g * group + ci
            io_rows = pl.ds(pl.multiple_of(c * CHUNK, CHUNK), CHUNK)
            rows = pl.ds(slot_row(g, ci), CHUNK)
            wq_rows = pl.ds(pl.multiple_of(slot_row(g, ci) * 2, 2 * CHUNK), 2 * CHUNK)
            r1s, v_news, outs = [], [], []
            for e in range(keys):
                sb = states[e].astype(BF16)
                r1s.append(jnp.dot(wq_s[e, wq_rows, :], _block_diag(sb[:, :DV], sb[:, DV:]),
                                   preferred_element_type=F32))
            yield
            for e in range(keys):
                v_news.append(u_s[e, rows, :] - r1s[e][:CHUNK])
            for e in range(keys):
                v_new = v_news[e]
                edg = edg_s[e, rows, :]
                edg_sw = pltpu.roll(edg, CHUNK, 1)
                dv = jnp.concatenate([v_new[:, :DV] * jnp.where(left, edg, edg_sw),
                                      v_new[:, DV:] * jnp.where(left, edg_sw, edg)], axis=1).astype(BF16)
                egl = jnp.exp(rows_ref[e, pl.ds(c, 1), 2 * PAIR_W:])
                states[e] = states[e] * egl + lax.dot_general(kb_s[e, rows, :], dv, tn_dims,
                                                              preferred_element_type=F32)
            for e in range(keys):
                vb = v_news[e].astype(BF16)
                outs.append(r1s[e][CHUNK:] + jnp.dot(qk_s[e, rows, :], _block_diag(vb[:, :DV], vb[:, DV:]),
                                                     preferred_element_type=F32))
            yield
            for e in range(keys):
                for j in range(HEADS_PER_KEY):
                    oj = outs[e][:, j * DV:(j + 1) * DV]
                    on = oj * lax.rsqrt(jnp.mean(oj * oj, axis=-1, keepdims=True) + EPS) * onorm
                    c0 = e * vw + j * DV
                    z = z_ref[io_rows, c0:c0 + DV].astype(F32)
                    o_ref[io_rows, c0:c0 + DV] = (on * (z * jax.nn.sigmoid(z))).astype(o_ref.dtype)
        for e in range(keys):
            st_s[e] = states[e]

    def run(local_gen, scan_gen):
        gens = [g for g in (local_gen, scan_gen) if g is not None]
        counts = {id(local_gen): LOCAL_STAGES, id(scan_gen): 2 * group + 1}
        order = sorted((float(s + 0.5 - 0.49 * n) / counts[id(g)], n, g) for n, g in enumerate(gens)
                       for s in range(counts[id(g)]))
        for _, _, g in order:
            next(g, None)
        for g in gens:
            assert next(g, "done") == "done"

    n_groups = n_chunks // group
    st_s[...] = jnp.zeros_like(st_s)
    run(local_stages(0), None)

    def merged(g, carry):
        run(local_stages(g + 1), scan_stages(g))
        return carry

    lax.fori_loop(0, n_groups - 1, merged, 0)
    run(None, scan_stages(n_groups - 1))


def _delta_rule(proj3, conv_w, layer, gcb, rows, out_norm, *, nk_heads, nv, keys=4, group=3):
    b, lp, _ = proj3.shape
    assert nv == HEADS_PER_KEY * nk_heads and nk_heads % keys == 0 and PAIR_W == 128
    assert (lp // CHUNK) % group == 0
    kw = keys * DK
    vw = keys * HEADS_PER_KEY * DV
    key_dim = nk_heads * DK
    k_blk = key_dim // kw
    v_blk = 2 * key_dim // vw
    z_blk = (2 * key_dim + nv * DV) // vw
    n_chunks = lp // CHUNK
    pvw = HEADS_PER_KEY * DV
    sr = 2 * group * CHUNK
    scratch = [pltpu.VMEM((keys, sr, pvw), F32),
               pltpu.VMEM((keys, 2 * sr, pvw), BF16),
               pltpu.VMEM((keys, sr, PAIR_W), BF16),
               pltpu.VMEM((keys, sr, DK), BF16),
               pltpu.VMEM((keys, sr, PAIR_W), F32),
               pltpu.VMEM((keys, DK, pvw), F32)]
    vmem = (2 * lp * (2 * kw + 3 * vw) * 2 + 2 * lp * 2 * nv * 4
            + keys * sr * (pvw * 4 + 2 * pvw * 2 + PAIR_W * 2 + DK * 2 + PAIR_W * 4) + 12 * MIB)
    return pl.pallas_call(
        functools.partial(_delta_kernel, nv=nv, keys=keys, group=group),
        grid=(b, nk_heads // keys),
        in_specs=[pl.BlockSpec((None, lp, kw), lambda bi, kp: (bi, 0, kp)),
                  pl.BlockSpec((None, lp, kw), lambda bi, kp: (bi, 0, k_blk + kp)),
                  pl.BlockSpec((None, lp, vw), lambda bi, kp: (bi, 0, v_blk + kp)),
                  pl.BlockSpec((None, lp, vw), lambda bi, kp: (bi, 0, z_blk + kp)),
                  pl.BlockSpec((None, CONV_K, kw), lambda bi, kp: (layer, 0, kp)),
                  pl.BlockSpec((None, CONV_K, kw), lambda bi, kp: (layer, 0, k_blk + kp)),
                  pl.BlockSpec((None, CONV_K, vw), lambda bi, kp: (layer, 0, v_blk + kp)),
                  pl.BlockSpec((None, lp, 2 * nv), lambda bi, kp: (bi, 0, 0)),
                  pl.BlockSpec((None, keys, n_chunks, rows.shape[-1]), lambda bi, kp: (bi, kp, 0, 0)),
                  pl.BlockSpec((1, DV), lambda bi, kp: (0, 0))],
        out_specs=pl.BlockSpec((None, lp, vw), lambda bi, kp: (bi, 0, kp)),
        out_shape=jax.ShapeDtypeStruct((b, lp, nv * DV), BF16),
        scratch_shapes=scratch,
        compiler_params=_params(vmem, 2),
        name="delta_rule",
    )(proj3, proj3, proj3, proj3, conv_w, conv_w, conv_w, gcb, rows, out_norm.reshape(1, DV))


def _pool_kernel(lb_ref, x_ref, g_ref, o_ref, *, col_blk):
    i = pl.program_id(1)
    tr, d = x_ref.shape
    gw = d // len(POOL_WINDOWS)
    x = x_ref[...]
    lb = lb_ref[...]
    inv = lax.rsqrt(jnp.mean(x * x, axis=-1, keepdims=True) + EPS)
    inv_lb = lax.rsqrt(jnp.mean(lb * lb, axis=-1, keepdims=True) + EPS)
    row = i * tr - LOOKBACK + lax.broadcasted_iota(jnp.int32, (tr + LOOKBACK, 1), 0)
    valid = row >= PAD
    pos = (row[LOOKBACK:] - (PAD - 1)).astype(F32)
    for gi, win in enumerate(POOL_WINDOWS):
        div = jnp.where(valid[LOOKBACK:], jnp.minimum(pos, float(win)), 1.0)
        for cb in range(gw // col_blk):
            c0 = gi * gw + cb * col_blk
            gv = g_ref[:, c0:c0 + col_blk]
            xe = jnp.concatenate([lb[:, c0:c0 + col_blk] * inv_lb * gv, x[:, c0:c0 + col_blk] * inv * gv], axis=0)
            xe = jnp.where(valid, xe, 0.0)
            s = xe
            sh = 1
            while sh < win:
                s = s + pltpu.roll(s, sh, 0)
                sh *= 2
            o_ref[:, c0:c0 + col_blk] = (s[LOOKBACK:] / div - xe[LOOKBACK:]).astype(o_ref.dtype)


def _pool(h3, g):
    b, lp, d = h3.shape
    tr = 3 * CHUNK
    assert lp % tr == 0
    per = tr // LOOKBACK
    return pl.pallas_call(
        functools.partial(_pool_kernel, col_blk=256),
        grid=(b, lp // tr),
        in_specs=[pl.BlockSpec((None, LOOKBACK, d), lambda bi, i: (bi, jnp.maximum(i * per - 1, 0), 0)),
                  pl.BlockSpec((None, tr, d), lambda bi, i: (bi, i, 0)),
                  pl.BlockSpec((1, d), lambda bi, i: (0, 0))],
        out_specs=pl.BlockSpec((None, tr, d), lambda bi, i: (bi, i, 0)),
        out_shape=jax.ShapeDtypeStruct((b, lp, d), BF16),
        compiler_params=_params(8 * tr * d * 4, 2),
        name="pool",
    )(h3, h3, g.reshape(1, d))


def kernel(x, meta_tokens, mix_norm, dn_w_in, dn_conv_w, dn_a_log, dn_dt_bias, dn_out_norm, dn_w_out, pool_w,
           pool_scale, mlp_norm, w_up, w_down, final_norm):
    b, seq, d = x.shape
    nv = dn_a_log.shape[1]
    val_dim = nv * DV
    key_dim = (dn_conv_w.shape[2] - val_dim) // 2
    nk_heads = key_dim // DK
    qkvz = 2 * key_dim + 2 * val_dim
    d_ff = w_up.shape[2]
    depth = mix_norm.shape[0]
    lp = PAD + N_META + seq
    m = b * lp
    tm = m // 8
    tm_stream = m // 4
    tm_chunk = m // 6
    assert lp % CHUNK == 0 and tm % 16 == 0 and tm_stream % 16 == 0 and tm_chunk % 16 == 0
    assert dn_w_in.shape[2] == qkvz + 2 * nv

    meta = jnp.broadcast_to(meta_tokens.astype(x.dtype)[None], (b, N_META, d))
    h = jnp.concatenate([jnp.zeros((b, PAD, d), x.dtype), meta, x], axis=1).reshape(m, d)

    for i in range(depth):
        j = i // 2
        if i % 2 == 0:
            hn = _rmsnorm(h, mix_norm[i], BF16)
            proj = _matmul_stream(hn, dn_w_in, j, n=qkvz, tn=512, tm=tm_stream, out_dtype=BF16)
            ba = _matmul_stream(hn, dn_w_in, j, n=2 * nv, n_off=qkvz, tn=2 * nv, tm=tm_stream, out_dtype=F32)
            gcb = _gates(ba.reshape(b, lp, 2 * nv), dn_a_log[j], dn_dt_bias[j])
            og = _delta_rule(proj.reshape(b, lp, qkvz), dn_conv_w, j, gcb, _gate_rows(gcb, nv), dn_out_norm[j],
                             nk_heads=nk_heads, nv=nv)
            h = _matmul_chunked(og.reshape(m, val_dim), dn_w_out, j, h, n=d, tn=512, tm=tm_chunk, kc=d)
        else:
            pooled = _pool(h.reshape(b, lp, d), mix_norm[i])
            h = _pool_matmul(pooled.reshape(m, d), pool_w, j, pool_scale[j], h, tm=tm)
        hn = _rmsnorm(h, mlp_norm[i], BF16)
        act = _matmul_stream(hn, w_up, i, n=d_ff, tn=512, tm=tm_stream, out_dtype=BF16, relu2=True)
        h = _matmul_chunked(act, w_down, i, h, n=d, tn=512, tm=tm_chunk, kc=d)

    return _final_norm(h.reshape(b, lp, d), final_norm, seq)
```

```python
import functools

import jax
import jax.numpy as jnp
from jax import lax
from jax.experimental import pallas as pl
from jax.experimental.pallas import tpu as pltpu

F32 = jnp.float32
BF16 = jnp.bfloat16

N_META = 16
CHUNK = 64
DK = 128
DV = 128
CONV_K = 4
POOL_WINDOWS = (2, 4, 8, 16)
EPS = 1e-6
PAD = (-N_META) % CHUNK
LOOKBACK = 16
HEADS_PER_KEY = 2
PAIR_W = HEADS_PER_KEY * CHUNK

V7X_VMEM_BYTES = 64 * 1024 * 1024
MIB = 1024 * 1024


def _params(vmem_bytes, n_axes):
    limit = min(int(vmem_bytes) + 4 * MIB, V7X_VMEM_BYTES - 6 * MIB)
    return pltpu.CompilerParams(dimension_semantics=("arbitrary",) * n_axes, vmem_limit_bytes=limit)


def _rmsnorm_kernel(x_ref, g_ref, o_ref):
    x = x_ref[...]
    ms = jnp.mean(x * x, axis=-1, keepdims=True)
    o_ref[...] = (x * lax.rsqrt(ms + EPS) * g_ref[...]).astype(o_ref.dtype)


def _rmsnorm(x, g, out_dtype, tr=256):
    m, d = x.shape
    return pl.pallas_call(
        _rmsnorm_kernel,
        grid=(m // tr,),
        in_specs=[pl.BlockSpec((tr, d), lambda i: (i, 0)), pl.BlockSpec((1, d), lambda i: (0, 0))],
        out_specs=pl.BlockSpec((tr, d), lambda i: (i, 0)),
        out_shape=jax.ShapeDtypeStruct((m, d), out_dtype),
        compiler_params=_params(6 * tr * d * 4, 1),
        name="rmsnorm",
    )(x, g.reshape(1, d))


def _final_norm_kernel(main_ref, next_ref, g_ref, o_ref):
    nb = o_ref.shape[0]
    g = g_ref[...]
    for c in range(nb):
        x = main_ref[c + 1] if c + 1 < nb else next_ref[0]
        ms = jnp.mean(x * x, axis=-1, keepdims=True)
        o_ref[c] = (x * lax.rsqrt(ms + EPS) * g).astype(o_ref.dtype)


def _final_norm(h3, g, seq, nb=8):
    b, lp, d = h3.shape
    assert lp - seq == CHUNK and seq % (nb * CHUNK) == 0
    h4 = h3.reshape(b, lp // CHUNK, CHUNK, d)
    out = pl.pallas_call(
        _final_norm_kernel,
        grid=(b, seq // (nb * CHUNK)),
        in_specs=[pl.BlockSpec((None, nb, CHUNK, d), lambda bi, i: (bi, i, 0, 0)),
                  pl.BlockSpec((None, 1, CHUNK, d), lambda bi, i: (bi, (i + 1) * nb, 0, 0)),
                  pl.BlockSpec((1, d), lambda bi, i: (0, 0))],
        out_specs=pl.BlockSpec((None, nb, CHUNK, d), lambda bi, i: (bi, i, 0, 0)),
        out_shape=jax.ShapeDtypeStruct((b, seq // CHUNK, CHUNK, d), h3.dtype),
        compiler_params=_params((4 * nb + 4) * CHUNK * d * 4, 2),
        name="final_norm",
    )(h4, h4, g.reshape(1, d))
    return out.reshape(b, seq, d)


def _pool_mm_kernel(a_ref, w_ref, res_ref, scale_ref, o_ref, wb_ref):
    @pl.when(pl.program_id(1) == 0)
    def _cast_weights():
        wb_ref[...] = w_ref[...].astype(BF16)

    y = jnp.dot(a_ref[...], wb_ref[...], preferred_element_type=F32)
    o_ref[...] = (res_ref[...] + y * scale_ref[...]).astype(o_ref.dtype)


def _mm_stream_kernel(*refs, relu2, has_prev):
    if has_prev:
        a_ref, w_ref, prev_ref, o_ref = refs
    else:
        a_ref, w_ref, o_ref = refs
    y = jnp.dot(a_ref[...], w_ref[...].astype(BF16), preferred_element_type=F32)
    if relu2:
        r = jnp.maximum(y, 0.0)
        y = r * r
    if has_prev:
        y = prev_ref[...] + y
    o_ref[...] = y.astype(o_ref.dtype)


def _matmul_stream(a, w, layer, *, n, tn, tm, out_dtype, n_off=0, relu2=False, prev=None, kc=None, k_blk=0,
                   a_buffers=1):
    m, kdim = a.shape
    kc = kdim if kc is None else kc
    assert m % tm == 0 and n % tn == 0 and n_off % tn == 0 and kdim % kc == 0
    joff = n_off // tn
    osz = jnp.dtype(out_dtype).itemsize
    vmem = (2 * kc * tn * 4 + kc * tn * 2 + a_buffers * tm * kc * 2 + 2 * tm * tn * osz + tm * tn * 4
            + (2 * tm * tn * 4 if prev is not None else 0))
    in_specs = [pl.BlockSpec((tm, kc), lambda i, j: (i, k_blk), pipeline_mode=pl.Buffered(a_buffers)),
                pl.BlockSpec((None, kc, tn), lambda i, j: (layer, k_blk, j + joff))]
    args = [a, w]
    if prev is not None:
        in_specs.append(pl.BlockSpec((tm, tn), lambda i, j: (i, j)))
        args.append(prev)
    return pl.pallas_call(
        functools.partial(_mm_stream_kernel, relu2=relu2, has_prev=prev is not None),
        grid=(m // tm, n // tn),
        in_specs=in_specs,
        out_specs=pl.BlockSpec((tm, tn), lambda i, j: (i, j)),
        out_shape=jax.ShapeDtypeStruct((m, n), out_dtype),
        compiler_params=_params(vmem, 2),
        name="matmul_stream",
    )(*args)


def _matmul_chunked(a, w, layer, res, *, n, tn, tm, kc):
    for k_blk in range(a.shape[1] // kc):
        res = _matmul_stream(a, w, layer, n=n, tn=tn, tm=tm, out_dtype=res.dtype, prev=res, kc=kc, k_blk=k_blk,
                             a_buffers=2)
    return res


def _pool_matmul(pooled, pool_w, layer, scale, res, *, tm):
    m, d = pooled.shape
    n_groups, gw = pool_w.shape[1], pool_w.shape[2]
    vmem = 2 * gw * gw * 4 + gw * gw * 2 + 2 * tm * gw * 2 + 6 * tm * gw * 4
    return pl.pallas_call(
        _pool_mm_kernel,
        grid=(n_groups, m // tm),
        in_specs=[pl.BlockSpec((tm, gw), lambda j, i: (i, j)),
                  pl.BlockSpec((None, None, gw, gw), lambda j, i: (layer, j, 0, 0)),
                  pl.BlockSpec((tm, gw), lambda j, i: (i, j)),
                  pl.BlockSpec((1, gw), lambda j, i: (0, j))],
        out_specs=pl.BlockSpec((tm, gw), lambda j, i: (i, j)),
        out_shape=jax.ShapeDtypeStruct((m, d), res.dtype),
        scratch_shapes=[pltpu.VMEM((gw, gw), BF16)],
        compiler_params=_params(vmem, 2),
        name="pool_matmul",
    )(pooled, pool_w, res, scale.reshape(1, d))


def _gate_kernel(ba_ref, alog_ref, dtb_ref, o_ref, *, nv):
    n_chunks = ba_ref.shape[0] // CHUNK
    lane = lax.broadcasted_iota(jnp.int32, (CHUNK, 2 * nv), 1)
    pos = lax.broadcasted_iota(jnp.int32, (CHUNK, 2 * nv), 0)
    neg_a = -jnp.exp(alog_ref[...])
    dtb = dtb_ref[...]

    def body(c, carry):
        rows = pl.ds(pl.multiple_of(c * CHUNK, CHUNK), CHUNK)
        x = ba_ref[rows, :]
        beta = jax.nn.sigmoid(x)
        xa = x + dtb
        g = neg_a * (jnp.maximum(xa, 0.0) + jnp.log1p(jnp.exp(-jnp.abs(xa))))
        val = jnp.where(lane < nv, beta, g)
        val = jnp.where(jnp.logical_or(c > 0, pos >= PAD), val, 0.0)
        cs = val
        s = 1
        while s < CHUNK:
            cs = cs + jnp.where(pos >= s, pltpu.roll(cs, s, 0), 0.0)
            s *= 2
        o_ref[rows, :] = jnp.where(lane < nv, val, cs)
        return carry

    lax.fori_loop(0, n_chunks, body, 0)


def _gates(ba3, a_log, dt_bias):
    b, lp, w = ba3.shape
    nv = w // 2
    zeros = jnp.zeros((nv,), F32)
    alog = jnp.concatenate([zeros, a_log.astype(F32)]).reshape(1, w)
    dtb = jnp.concatenate([zeros, dt_bias.astype(F32)]).reshape(1, w)
    return pl.pallas_call(
        functools.partial(_gate_kernel, nv=nv),
        grid=(b,),
        in_specs=[pl.BlockSpec((None, lp, w), lambda bi: (bi, 0, 0)),
                  pl.BlockSpec((1, w), lambda bi: (0, 0)),
                  pl.BlockSpec((1, w), lambda bi: (0, 0))],
        out_specs=pl.BlockSpec((None, lp, w), lambda bi: (bi, 0, 0)),
        out_shape=jax.ShapeDtypeStruct((b, lp, w), F32),
        compiler_params=_params(8 * lp * w * 4, 1),
        name="delta_gates",
    )(ba3, alog, dtb)


def _gate_rows(gcb, nv):
    b, lp, _ = gcb.shape
    nkh = nv // HEADS_PER_KEY
    nc = lp // CHUNK
    gcr = gcb[:, :, nv:].reshape(b, nc, CHUNK, nkh, HEADS_PER_KEY).transpose(0, 3, 1, 4, 2)
    gl = gcr[..., CHUNK - 1:]
    return jnp.concatenate(
        [gcr.reshape(b, nkh, nc, PAIR_W),
         jnp.broadcast_to(gl, (b, nkh, nc, HEADS_PER_KEY, CHUNK)).reshape(b, nkh, nc, PAIR_W),
         jnp.broadcast_to(gl, (b, nkh, nc, HEADS_PER_KEY, DV)).reshape(b, nkh, nc, HEADS_PER_KEY * DV)], axis=-1)


def _block_diag(a, b):
    z = jnp.zeros_like(a)
    return jnp.concatenate([jnp.concatenate([a, z], axis=1), jnp.concatenate([z, b], axis=1)], axis=0)


def _delta_kernel(q_ref, k_ref, v_ref, z_ref, cwq_ref, cwk_ref, cwv_ref, gcb_ref, rows_ref, onorm_ref,
                  o_ref, u_s, wq_s, qk_s, kb_s, edg_s, st_s, *, nv, keys, group):
    kp = pl.program_id(1)
    n_chunks = q_ref.shape[0] // CHUNK
    vw = HEADS_PER_KEY * DV
    rowi = lax.broadcasted_iota(jnp.int32, (CHUNK, PAIR_W), 0)
    lanei = lax.broadcasted_iota(jnp.int32, (CHUNK, PAIR_W), 1)
    coli = lanei & (CHUNK - 1)
    left = lanei < CHUNK
    lower = rowi >= coli
    strict = rowi > coli
    eye = jnp.where(rowi == coli, 1.0, 0.0).astype(F32)
    lane_g = lax.broadcasted_iota(jnp.int32, (CHUNK, 2 * nv), 1)
    nt_dims = (((1,), (1,)), ((), ()))
    tn_dims = (((0,), (0,)), ((), ()))

    def mm_pair(x, y):
        ybd = jnp.concatenate([jnp.where(left, y, 0.0), jnp.where(left, 0.0, y)], axis=0).astype(BF16)
        return jnp.dot(x.astype(BF16), ybd, preferred_element_type=F32)

    def l2n(t, scale=1.0):
        return t * (lax.rsqrt(jnp.sum(t * t, axis=-1, keepdims=True) + EPS) * scale)

    def conv_silu(ref, cw_ref, rows, lb_rows, c0, width):
        x = jnp.concatenate([ref[lb_rows, c0:c0 + width], ref[rows, c0:c0 + width]], axis=0).astype(F32)
        w = cw_ref[:, c0:c0 + width]
        y = x[LOOKBACK:] * w[CONV_K - 1:CONV_K]
        for j in range(1, CONV_K):
            y = y + x[LOOKBACK - j:LOOKBACK - j + CHUNK] * w[CONV_K - 1 - j:CONV_K - j]
        return y * jax.nn.sigmoid(y)

    LOCAL_STAGES = 10

    def slot_row(slot, ci):
        return (slot * group + ci) * CHUNK

    def local_stages(g, slot):
        chains = []
        for ci in range(group):
            c = g * group + ci
            r0 = pl.multiple_of(c * CHUNK, CHUNK)
            lb_rows = pl.ds(pl.multiple_of(jnp.maximum(r0 - LOOKBACK, 0), LOOKBACK), LOOKBACK)
            for e in range(keys):
                chains.append(dict(c=c, e=e, rows=pl.ds(r0, CHUNK), lb_rows=lb_rows, s0=slot_row(slot, ci)))

        for ch in chains:
            e, rows, lb_rows = ch["e"], ch["rows"], ch["lb_rows"]
            q = l2n(conv_silu(q_ref, cwq_ref, rows, lb_rows, e * DK, DK), DK ** -0.5)
            k = l2n(conv_silu(k_ref, cwk_ref, rows, lb_rows, e * DK, DK))
            qb = q.astype(BF16)
            kb = k.astype(BF16)
            qkk = lax.dot_general(jnp.concatenate([qb, kb], axis=0), jnp.concatenate([kb, kb], axis=0),
                                  nt_dims, preferred_element_type=F32)
            ch.update(q=q, k=k, kb=kb, qkk=qkk)
        yield

        for ch in chains:
            gcb = gcb_ref[ch["rows"], :]

            def column(idx, gcb=gcb):
                return jnp.sum(jnp.where(lane_g == idx, gcb, 0.0), axis=1, keepdims=True)

            h0 = HEADS_PER_KEY * (keys * kp + ch["e"])
            beta_a, beta_b = column(h0), column(h0 + 1)
            gc_a, gc_b = column(nv + h0), column(nv + h0 + 1)
            gc = jnp.where(left, gc_a, gc_b)
            rr = rows_ref[ch["e"], pl.ds(ch["c"], 1), :]
            decay = jnp.where(lower, jnp.exp(jnp.where(lower, gc - rr[:, :PAIR_W], 0.0)), 0.0)
            n = -jnp.where(strict, (jnp.where(left, beta_a, beta_b) * ch["qkk"][CHUNK:]) * decay, 0.0)
            ch.update(beta_a=beta_a, beta_b=beta_b, eg_a=jnp.exp(gc_a), eg_b=jnp.exp(gc_b), decay=decay,
                      edg=jnp.exp(rr[:, PAIR_W:2 * PAIR_W] - gc), t=eye + n, p=n)

        assert CHUNK == 64

        def square_and_extend(p, x):
            both = mm_pair(jnp.concatenate([p, x], axis=0), p)
            return both[:CHUNK], x + both[CHUNK:]

        for ch in chains:
            ch["p2"] = mm_pair(ch["p"], ch["p"])
        yield
        for ch in chains:
            ch["p4"], ch["a1"] = square_and_extend(ch["p2"], ch["t"])
        yield
        for ch in chains:
            ch["p8"] = mm_pair(ch["p4"], ch["p4"])
        yield
        for ch in chains:
            ch["p16"], ch["a2"] = square_and_extend(ch["p8"], eye + ch["p4"])
        yield
        for ch in chains:
            ch["p32"] = mm_pair(ch["p16"], ch["p16"])
        for ch in chains:
            ch["a12"] = mm_pair(ch["a1"], ch["a2"])
        yield
        for ch in chains:
            x = eye + ch["p16"]
            ch["a3"] = x + mm_pair(x, ch["p32"])
        yield
        for ch in chains:
            ch["t"] = mm_pair(ch["a12"], ch["a3"])
        yield

        for ch in chains:
            e, rows = ch["e"], ch["rows"]
            v = conv_silu(v_ref, cwv_ref, rows, ch["lb_rows"], e * vw, vw)
            k = ch["k"]
            rhs_a = jnp.concatenate([v[:, :DV] * ch["beta_a"], k * (ch["beta_a"] * ch["eg_a"])], axis=1)
            rhs_b = jnp.concatenate([v[:, DV:] * ch["beta_b"], k * (ch["beta_b"] * ch["eg_b"])], axis=1)
            ch["uw"] = jnp.dot(ch["t"].astype(BF16), _block_diag(rhs_a.astype(BF16), rhs_b.astype(BF16)),
                               preferred_element_type=F32)
        yield

        for ch in chains:
            e, uw, q = ch["e"], ch["uw"], ch["q"]
            rows = pl.ds(ch["s0"], CHUNK)
            wq0 = ch["s0"] * 2
            u_s[e, rows, :] = jnp.concatenate([uw[:, :DV], uw[:, 2 * DV:3 * DV]], axis=1)
            wq_s[e, pl.ds(wq0, CHUNK), :] = jnp.concatenate([uw[:, DV:2 * DV], uw[:, 3 * DV:]], axis=1).astype(BF16)
            wq_s[e, pl.ds(wq0 + CHUNK, CHUNK), :] = jnp.concatenate([q * ch["eg_a"], q * ch["eg_b"]],
                                                                     axis=1).astype(BF16)
            qk_s[e, rows, :] = (ch["qkk"][:CHUNK] * ch["decay"]).astype(BF16)
            kb_s[e, rows, :] = ch["kb"]
            edg_s[e, rows, :] = ch["edg"]

    onorm = onorm_ref[...]

    def scan_stages(g, slot):
        states = [st_s[e] for e in range(keys)]
        for ci in range(group):
            c = g * group + ci
            io_rows = pl.ds(pl.multiple_of(c * CHUNK, CHUNK), CHUNK)
            rows = pl.ds(slot_row(slot, ci), CHUNK)
            wq_rows = pl.ds(slot_row(slot, ci) * 2, 2 * CHUNK)
            r1s, v_news, outs = [], [], []
            for e in range(keys):
                sb = states[e].astype(BF16)
                r1s.append(jnp.dot(wq_s[e, wq_rows, :], _block_diag(sb[:, :DV], sb[:, DV:]),
                                   preferred_element_type=F32))
            yield
            for e in range(keys):
                v_news.append(u_s[e, rows, :] - r1s[e][:CHUNK])
            for e in range(keys):
                v_new = v_news[e]
                edg = edg_s[e, rows, :]
                edg_sw = pltpu.roll(edg, CHUNK, 1)
                dv = jnp.concatenate([v_new[:, :DV] * jnp.where(left, edg, edg_sw),
                                      v_new[:, DV:] * jnp.where(left, edg_sw, edg)], axis=1).astype(BF16)
                egl = jnp.exp(rows_ref[e, pl.ds(c, 1), 2 * PAIR_W:])
                states[e] = states[e] * egl + lax.dot_general(kb_s[e, rows, :], dv, tn_dims,
                                                              preferred_element_type=F32)
            for e in range(keys):
                vb = v_news[e].astype(BF16)
                outs.append(r1s[e][CHUNK:] + jnp.dot(qk_s[e, rows, :], _block_diag(vb[:, :DV], vb[:, DV:]),
                                                     preferred_element_type=F32))
            yield
            for e in range(keys):
                for j in range(HEADS_PER_KEY):
                    oj = outs[e][:, j * DV:(j + 1) * DV]
                    on = oj * lax.rsqrt(jnp.mean(oj * oj, axis=-1, keepdims=True) + EPS) * onorm
                    c0 = e * vw + j * DV
                    z = z_ref[io_rows, c0:c0 + DV].astype(F32)
                    o_ref[io_rows, c0:c0 + DV] = (on * (z * jax.nn.sigmoid(z))).astype(o_ref.dtype)
        for e in range(keys):
            st_s[e] = states[e]

    def run(local_gen, scan_gen):
        gens = [g for g in (local_gen, scan_gen) if g is not None]
        counts = {id(local_gen): LOCAL_STAGES, id(scan_gen): 2 * group + 1}
        order = sorted((float(s + 0.5 - 0.49 * n) / counts[id(g)], n, g) for n, g in enumerate(gens)
                       for s in range(counts[id(g)]))
        for _, _, g in order:
            next(g, None)
        for g in gens:
            assert next(g, "done") == "done"

    n_groups = n_chunks // group
    st_s[...] = jnp.zeros_like(st_s)
    run(local_stages(0, 0), None)

    assert n_groups % 2 == 1

    def merged(p, carry):
        run(local_stages(2 * p + 1, 1), scan_stages(2 * p, 0))
        run(local_stages(2 * p + 2, 0), scan_stages(2 * p + 1, 1))
        return carry

    lax.fori_loop(0, (n_groups - 1) // 2, merged, 0)
    run(None, scan_stages(n_groups - 1, 0))


def _delta_rule(proj3, conv_w, layer, gcb, rows, out_norm, *, nk_heads, nv, keys=4, group=3):
    b, lp, _ = proj3.shape
    assert nv == HEADS_PER_KEY * nk_heads and nk_heads % keys == 0 and PAIR_W == 128
    assert (lp // CHUNK) % group == 0
    kw = keys * DK
    vw = keys * HEADS_PER_KEY * DV
    key_dim = nk_heads * DK
    k_blk = key_dim // kw
    v_blk = 2 * key_dim // vw
    z_blk = (2 * key_dim + nv * DV) // vw
    n_chunks = lp // CHUNK
    pvw = HEADS_PER_KEY * DV
    sr = 2 * group * CHUNK
    scratch = [pltpu.VMEM((keys, sr, pvw), F32),
               pltpu.VMEM((keys, 2 * sr, pvw), BF16),
               pltpu.VMEM((keys, sr, PAIR_W), BF16),
               pltpu.VMEM((keys, sr, DK), BF16),
               pltpu.VMEM((keys, sr, PAIR_W), F32),
               pltpu.VMEM((keys, DK, pvw), F32)]
    vmem = (2 * lp * (2 * kw + 3 * vw) * 2 + 2 * lp * 2 * nv * 4
            + keys * sr * (pvw * 4 + 2 * pvw * 2 + PAIR_W * 2 + DK * 2 + PAIR_W * 4) + 12 * MIB)
    return pl.pallas_call(
        functools.partial(_delta_kernel, nv=nv, keys=keys, group=group),
        grid=(b, nk_heads // keys),
        in_specs=[pl.BlockSpec((None, lp, kw), lambda bi, kp: (bi, 0, kp)),
                  pl.BlockSpec((None, lp, kw), lambda bi, kp: (bi, 0, k_blk + kp)),
                  pl.BlockSpec((None, lp, vw), lambda bi, kp: (bi, 0, v_blk + kp)),
                  pl.BlockSpec((None, lp, vw), lambda bi, kp: (bi, 0, z_blk + kp)),
                  pl.BlockSpec((None, CONV_K, kw), lambda bi, kp: (layer, 0, kp)),
                  pl.BlockSpec((None, CONV_K, kw), lambda bi, kp: (layer, 0, k_blk + kp)),
                  pl.BlockSpec((None, CONV_K, vw), lambda bi, kp: (layer, 0, v_blk + kp)),
                  pl.BlockSpec((None, lp, 2 * nv), lambda bi, kp: (bi, 0, 0)),
                  pl.BlockSpec((None, keys, n_chunks, rows.shape[-1]), lambda bi, kp: (bi, kp, 0, 0)),
                  pl.BlockSpec((1, DV), lambda bi, kp: (0, 0))],
        out_specs=pl.BlockSpec((None, lp, vw), lambda bi, kp: (bi, 0, kp)),
        out_shape=jax.ShapeDtypeStruct((b, lp, nv * DV), BF16),
        scratch_shapes=scratch,
        compiler_params=_params(vmem, 2),
        name="delta_rule",
    )(proj3, proj3, proj3, proj3, conv_w, conv_w, conv_w, gcb, rows, out_norm.reshape(1, DV))


def _pool_kernel(lb_ref, x_ref, g_ref, o_ref, *, col_blk):
    i = pl.program_id(1)
    tr, d = x_ref.shape
    gw = d // len(POOL_WINDOWS)
    x = x_ref[...]
    lb = lb_ref[...]
    inv = lax.rsqrt(jnp.mean(x * x, axis=-1, keepdims=True) + EPS)
    inv_lb = lax.rsqrt(jnp.mean(lb * lb, axis=-1, keepdims=True) + EPS)
    row = i * tr - LOOKBACK + lax.broadcasted_iota(jnp.int32, (tr + LOOKBACK, 1), 0)
    valid = row >= PAD
    pos = (row[LOOKBACK:] - (PAD - 1)).astype(F32)
    for gi, win in enumerate(POOL_WINDOWS):
        div = jnp.where(valid[LOOKBACK:], jnp.minimum(pos, float(win)), 1.0)
        for cb in range(gw // col_blk):
            c0 = gi * gw + cb * col_blk
            gv = g_ref[:, c0:c0 + col_blk]
            xe = jnp.concatenate([lb[:, c0:c0 + col_blk] * inv_lb * gv, x[:, c0:c0 + col_blk] * inv * gv], axis=0)
            xe = jnp.where(valid, xe, 0.0)
            s = xe
            sh = 1
            while sh < win:
                s = s + pltpu.roll(s, sh, 0)
                sh *= 2
            o_ref[:, c0:c0 + col_blk] = (s[LOOKBACK:] / div - xe[LOOKBACK:]).astype(o_ref.dtype)


def _pool(h3, g):
    b, lp, d = h3.shape
    tr = 3 * CHUNK
    assert lp % tr == 0
    per = tr // LOOKBACK
    return pl.pallas_call(
        functools.partial(_pool_kernel, col_blk=256),
        grid=(b, lp // tr),
        in_specs=[pl.BlockSpec((None, LOOKBACK, d), lambda bi, i: (bi, jnp.maximum(i * per - 1, 0), 0)),
                  pl.BlockSpec((None, tr, d), lambda bi, i: (bi, i, 0)),
                  pl.BlockSpec((1, d), lambda bi, i: (0, 0))],
        out_specs=pl.BlockSpec((None, tr, d), lambda bi, i: (bi, i, 0)),
        out_shape=jax.ShapeDtypeStruct((b, lp, d), BF16),
        compiler_params=_params(8 * tr * d * 4, 2),
        name="pool",
    )(h3, h3, g.reshape(1, d))


def kernel(x, meta_tokens, mix_norm, dn_w_in, dn_conv_w, dn_a_log, dn_dt_bias, dn_out_norm, dn_w_out, pool_w,
           pool_scale, mlp_norm, w_up, w_down, final_norm):
    b, seq, d = x.shape
    nv = dn_a_log.shape[1]
    val_dim = nv * DV
    key_dim = (dn_conv_w.shape[2] - val_dim) // 2
    nk_heads = key_dim // DK
    qkvz = 2 * key_dim + 2 * val_dim
    d_ff = w_up.shape[2]
    depth = mix_norm.shape[0]
    lp = PAD + N_META + seq
    m = b * lp
    tm = m // 8
    tm_stream = m // 4
    tm_chunk = m // 6
    assert lp % CHUNK == 0 and tm % 16 == 0 and tm_stream % 16 == 0 and tm_chunk % 16 == 0
    assert dn_w_in.shape[2] == qkvz + 2 * nv

    meta = jnp.broadcast_to(meta_tokens.astype(x.dtype)[None], (b, N_META, d))
    h = jnp.concatenate([jnp.zeros((b, PAD, d), x.dtype), meta, x], axis=1).reshape(m, d)

    for i in range(depth):
        j = i // 2
        if i % 2 == 0:
            hn = _rmsnorm(h, mix_norm[i], BF16)
            proj = _matmul_stream(hn, dn_w_in, j, n=qkvz, tn=512, tm=tm_stream, out_dtype=BF16)
            ba = _matmul_stream(hn, dn_w_in, j, n=2 * nv, n_off=qkvz, tn=2 * nv, tm=tm_stream, out_dtype=F32)
            gcb = _gates(ba.reshape(b, lp, 2 * nv), dn_a_log[j], dn_dt_bias[j])
            og = _delta_rule(proj.reshape(b, lp, qkvz), dn_conv_w, j, gcb, _gate_rows(gcb, nv), dn_out_norm[j],
                             nk_heads=nk_heads, nv=nv)
            h = _matmul_chunked(og.reshape(m, val_dim), dn_w_out, j, h, n=d, tn=512, tm=tm_chunk, kc=d)
        else:
            pooled = _pool(h.reshape(b, lp, d), mix_norm[i])
            h = _pool_matmul(pooled.reshape(m, d), pool_w, j, pool_scale[j], h, tm=tm)
        hn = _rmsnorm(h, mlp_norm[i], BF16)
        act = _matmul_stream(hn, w_up, i, n=d_ff, tn=512, tm=tm_stream, out_dtype=BF16, relu2=True)
        h = _matmul_chunked(act, w_down, i, h, n=d, tn=512, tm=tm_chunk, kc=d)

    return _final_norm(h.reshape(b, lp, d), final_norm, seq)
```
